```python
import jax, jax.numpy as jnp
from jax import lax
import numpy as np

D_MODEL = 1024
BATCH = 4
SEQ = 8192
DEPTH = 2

HEAD_DIM = 64
N_MLSTM_HEADS = D_MODEL // (2 * HEAD_DIM)
N_SB_HEADS = D_MODEL // (2 * HEAD_DIM)
D_MLSTM = N_MLSTM_HEADS * HEAD_DIM
D_SB = N_SB_HEADS * HEAD_DIM
D_MIX = D_MLSTM + D_SB
D_FF = 4 * D_MODEL
CONV_WIDTH = 4
MLSTM_CHUNK = 64
SB_BLOCK = 128
NORM_EPS = 1e-6
D_IN_PROJ = 4 * D_MLSTM + 2 * N_MLSTM_HEADS + 3 * D_SB

kernel_name = "hymba_mlstm_stickbreaking_trunk"


def _rmsnorm(x, g):
    xf = x.astype(jnp.float32)
    y = xf * lax.rsqrt(jnp.mean(xf * xf, axis=-1, keepdims=True) + NORM_EPS)
    return y * g.astype(jnp.float32)


def _head_rmsnorm(y, g):
    B, S, H, d = y.shape
    return _rmsnorm(y, g.reshape(H, d))


def _causal_depthwise_conv(u, w, b):
    S = u.shape[1]
    up = jnp.pad(u, ((0, 0), (CONV_WIDTH - 1, 0), (0, 0)))
    y = b
    for j in range(CONV_WIDTH):
        y = y + up[:, j:j + S, :] * w[j]
    return y


def _to_heads(a, H):
    B, S, _ = a.shape
    return a.reshape(B, S, H, HEAD_DIM).transpose(0, 2, 1, 3).astype(jnp.float32)


def _mlstm_chunkwise(q, k, v, i_pre, f_pre):
    B, H, S, d = q.shape
    L = MLSTM_CHUNK
    nc = S // L
    q = q * (d ** -0.5)
    logf = jax.nn.log_sigmoid(f_pre)

    def to_chunks(a):
        return jnp.moveaxis(a.reshape(B, H, nc, L, *a.shape[3:]), 2, 0)

    tri = jnp.tril(jnp.ones((L, L), dtype=bool))

    def step(carry, inp):
        C, n, m_prev = carry
        qc, kc, vc, ic, fc = inp
        b = jnp.cumsum(fc, axis=-1)
        Dlog = jnp.where(tri, b[..., :, None] - b[..., None, :] + ic[..., None, :], -jnp.inf)
        m_inter = b + m_prev[..., None]
        m = jnp.maximum(m_inter, jnp.max(Dlog, axis=-1))
        W = jnp.exp(Dlog - m[..., None]) * jnp.einsum('bhtd,bhsd->bhts', qc, kc)
        s_inter = jnp.exp(m_inter - m)
        num = s_inter[..., None] * jnp.einsum('bhtk,bhkv->bhtv', qc, C) + jnp.einsum('bhts,bhsv->bhtv', W, vc)
        den = s_inter * jnp.einsum('bhtk,bhk->bht', qc, n) + jnp.sum(W, axis=-1)
        h = num / jnp.maximum(jnp.abs(den), jnp.exp(-m))[..., None]
        b_last = b[..., -1]
        g = b_last[..., None] - b + ic
        m_new = jnp.maximum(b_last + m_prev, jnp.max(g, axis=-1))
        decay = jnp.exp(b_last + m_prev - m_new)
        w_s = jnp.exp(g - m_new[..., None])
        C_new = decay[..., None, None] * C + jnp.einsum('bhs,bhsk,bhsv->bhkv', w_s, kc, vc)
        n_new = decay[..., None] * n + jnp.einsum('bhs,bhsk->bhk', w_s, kc)
        return (C_new, n_new, m_new), h

    init = (jnp.zeros((B, H, d, d), jnp.float32), jnp.zeros((B, H, d), jnp.float32),
            jnp.zeros((B, H), jnp.float32))
    _, hs = lax.scan(step, init, (to_chunks(q), to_chunks(k), to_chunks(v), to_chunks(i_pre), to_chunks(logf)))
    return jnp.moveaxis(hs, 0, 2).reshape(B, H, S, d)


def _stick_breaking(q, k, v):
    B, H, S, d = q.shape
    scale = d ** -0.5
    outs = []
    for blk in range(S // SB_BLOCK):
        t0 = blk * SB_BLOCK
        t1 = t0 + SB_BLOCK
        qb = q[:, :, t0:t1]
        kb = k[:, :, :t1]
        vb = v[:, :, :t1]
        z = jnp.einsum('bhtd,bhsd->bhts', qb, kb) * scale
        causal = jnp.arange(t1)[None, :] < (t0 + jnp.arange(SB_BLOCK))[:, None]
        log_1m_beta = jnp.where(causal, jax.nn.log_sigmoid(-z), 0.0)
        after = lax.cumsum(log_1m_beta, axis=3, reverse=True) - log_1m_beta
        A = jnp.where(causal, jnp.exp(jax.nn.log_sigmoid(z) + after), 0.0)
        outs.append(jnp.einsum('bhts,bhsv->bhtv', A, vb))
    return jnp.concatenate(outs, axis=2)


def setup_inputs(seed: int = 0) -> dict:
    key = jax.random.key(seed)
    ks = jax.random.split(key, 15)
    f32 = jnp.float32
    x = jax.random.normal(ks[0], (BATCH, SEQ, D_MODEL), f32)
    attn_norm_g = 1.0 + 0.02 * jax.random.normal(ks[1], (DEPTH, D_MODEL), f32)
    w_in = jax.random.normal(ks[2], (DEPTH, D_MODEL, D_IN_PROJ), f32) * D_MODEL ** -0.5
    conv_w = jax.random.normal(ks[3], (DEPTH, CONV_WIDTH, 2 * D_MLSTM), f32) * CONV_WIDTH ** -0.5
    conv_b = 0.02 * jax.random.normal(ks[4], (DEPTH, 2 * D_MLSTM), f32)
    b_igate = 0.1 * jax.random.normal(ks[5], (DEPTH, N_MLSTM_HEADS), f32)
    b_fgate = jnp.linspace(3.0, 6.0, N_MLSTM_HEADS, dtype=f32)[None, :] + 0.1 * jax.random.normal(ks[6], (DEPTH, N_MLSTM_HEADS), f32)
    q_norm_g = 1.0 + 0.02 * jax.random.normal(ks[7], (DEPTH, HEAD_DIM), f32)
    k_norm_g = 1.0 + 0.02 * jax.random.normal(ks[8], (DEPTH, HEAD_DIM), f32)
    out_norm_g = 1.0 + 0.02 * jax.random.normal(ks[9], (DEPTH, D_MIX), f32)
    w_out = jax.random.normal(ks[10], (DEPTH, D_MIX, D_MODEL), f32) * D_MIX ** -0.5
    mlp_norm_g = 1.0 + 0.02 * jax.random.normal(ks[11], (DEPTH, D_MODEL), f32)
    w_up = jax.random.normal(ks[12], (DEPTH, D_MODEL, D_FF), f32) * D_MODEL ** -0.5
    w_down = jax.random.normal(ks[13], (DEPTH, D_FF, D_MODEL), f32) * D_FF ** -0.5
    return {"x": x, "attn_norm_g": attn_norm_g, "w_in": w_in, "conv_w": conv_w, "conv_b": conv_b,
            "b_igate": b_igate, "b_fgate": b_fgate, "q_norm_g": q_norm_g, "k_norm_g": k_norm_g,
            "out_norm_g": out_norm_g, "w_out": w_out, "mlp_norm_g": mlp_norm_g, "w_up": w_up, "w_down": w_down}


def reference(x, attn_norm_g, w_in, conv_w, conv_b, b_igate, b_fgate, q_norm_g, k_norm_g,
              out_norm_g, w_out, mlp_norm_g, w_up, w_down):
    B, S, _ = x.shape
    o_qk, o_v, o_o, o_i, o_f = 0, 2 * D_MLSTM, 3 * D_MLSTM, 4 * D_MLSTM, 4 * D_MLSTM + N_MLSTM_HEADS
    o_sb = 4 * D_MLSTM + 2 * N_MLSTM_HEADS
    for l in range(DEPTH):
        h = _rmsnorm(x, attn_norm_g[l]).astype(x.dtype)
        p = h @ w_in[l]
        qk_m = jax.nn.silu(_causal_depthwise_conv(p[..., o_qk:o_v], conv_w[l], conv_b[l]))
        q_m = _to_heads(qk_m[..., :D_MLSTM], N_MLSTM_HEADS)
        k_m = _to_heads(qk_m[..., D_MLSTM:], N_MLSTM_HEADS)
        v_m = _to_heads(p[..., o_v:o_o], N_MLSTM_HEADS)
        o_gate = jax.nn.sigmoid(p[..., o_o:o_i].astype(jnp.float32))
        i_pre = (p[..., o_i:o_f] + b_igate[l]).astype(jnp.float32).transpose(0, 2, 1)
        f_pre = (p[..., o_f:o_sb] + b_fgate[l]).astype(jnp.float32).transpose(0, 2, 1)
        y_m = _mlstm_chunkwise(q_m, k_m, v_m, i_pre, f_pre).transpose(0, 2, 1, 3)
        y_m = _head_rmsnorm(y_m, out_norm_g[l, :D_MLSTM]).reshape(B, S, D_MLSTM) * o_gate
        q_s = _rmsnorm(_to_heads(p[..., o_sb:o_sb + D_SB], N_SB_HEADS), q_norm_g[l])
        k_s = _rmsnorm(_to_heads(p[..., o_sb + D_SB:o_sb + 2 * D_SB], N_SB_HEADS), k_norm_g[l])
        v_s = _to_heads(p[..., o_sb + 2 * D_SB:], N_SB_HEADS)
        y_s = _stick_breaking(q_s, k_s, v_s).transpose(0, 2, 1, 3)
        y_s = _head_rmsnorm(y_s, out_norm_g[l, D_MLSTM:]).reshape(B, S, D_SB)
        y = jnp.concatenate([y_m, y_s], axis=-1).astype(x.dtype)
        x = x + y @ w_out[l]
        h = _rmsnorm(x, mlp_norm_g[l]).astype(x.dtype)
        x = x + jnp.square(jax.nn.relu(h @ w_up[l])) @ w_down[l]
    return x
```

```python
import functools

import jax
import jax.numpy as jnp
from jax import lax
from jax.experimental import pallas as pl
from jax.experimental.pallas import tpu as pltpu

D_MODEL = 1024
HEAD_DIM = 64
N_HEADS = 8
D_GROUP = N_HEADS * HEAD_DIM
N_PAIRS = N_HEADS // 2
D_FF = 4 * D_MODEL
CONV_WIDTH = 4
NORM_EPS = 1e-6
LANES = 128
D_PROJ = 7 * D_GROUP + 2 * LANES

TOK_TILE = 512
FF_TILE = 512
MLSTM_TILE = 512
MLSTM_CHUNK = 128
SB_TILE = 256
VMEM_LIMIT = 56 * 1024 * 1024

_F32 = jnp.float32
_BF16 = jnp.bfloat16
_NEG = -1e30


def _dot(a, b):
    return jnp.dot(a, b, preferred_element_type=_F32)


def _dot_nt(a, b):
    return lax.dot_general(a, b, (((1,), (1,)), ((), ())), preferred_element_type=_F32)


def _dot_tn(a, b):
    return lax.dot_general(a, b, (((0,), (0,)), ((), ())), preferred_element_type=_F32)


def _split_bf16(a, pieces):
    parts = []
    r = a
    for i in range(pieces):
        p = r.astype(_BF16)
        parts.append(p)
        if i + 1 < pieces:
            r = r - p.astype(_F32)
    return parts


def _softplus(z):
    return jnp.maximum(z, 0.0) + jnp.log(1.0 + jnp.exp(-jnp.abs(z)))


def _pair_rmsnorm(x, gain):
    lane = lax.broadcasted_iota(jnp.int32, x.shape, 1)
    first = lane < HEAD_DIM
    sq = x * x
    s_a = jnp.sum(jnp.where(first, sq, 0.0), axis=-1, keepdims=True)
    s_b = jnp.sum(jnp.where(first, 0.0, sq), axis=-1, keepdims=True)
    ms = jnp.where(first, s_a, s_b) * (1.0 / HEAD_DIM)
    return x * lax.rsqrt(ms + NORM_EPS) * gain


def _in_proj_kernel(x_ref, g_ref, w_ref, qg_ref, kg_ref,
                    qk_ref, vm_ref, om_ref, qs_ref, ks_ref, vs_ref, gate_ref):
    x = x_ref[...]
    ms = jnp.mean(x * x, axis=-1, keepdims=True)
    h = (x * lax.rsqrt(ms + NORM_EPS) * g_ref[...]).astype(_BF16)

    def proj(c):
        return _dot(h, w_ref[:, c * D_GROUP:(c + 1) * D_GROUP])

    qk_ref[:, 0:D_GROUP] = proj(0)
    qk_ref[:, D_GROUP:2 * D_GROUP] = proj(1)
    vm_ref[...] = proj(2).astype(_BF16)
    om_ref[...] = proj(3)
    for c, gain_ref, out_ref in ((4, qg_ref, qs_ref), (5, kg_ref, ks_ref)):
        r = proj(c)
        for p in range(N_PAIRS):
            sl = slice(p * LANES, (p + 1) * LANES)
            out_ref[:, sl] = _pair_rmsnorm(r[:, sl], gain_ref[...]).astype(_BF16)
    vs_ref[...] = proj(6).astype(_BF16)
    gate_ref[...] = _dot(h, w_ref[:, 7 * D_GROUP:D_PROJ])


def _in_proj(x2d, g, w, qg, kg):
    n = x2d.shape[0]
    tn = TOK_TILE
    row = lambda width: pl.BlockSpec((tn, width), lambda i: (i, 0))
    full = lambda shape: pl.BlockSpec(shape, lambda i: (0, 0))
    return pl.pallas_call(
        _in_proj_kernel,
        grid=(n // tn,),
        in_specs=[row(D_MODEL), full((1, D_MODEL)), full((D_MODEL, D_PROJ)),
                  full((1, LANES)), full((1, LANES))],
        out_specs=[row(2 * D_GROUP), row(D_GROUP), row(D_GROUP), row(D_GROUP), row(D_GROUP),
                   row(D_GROUP), row(2 * LANES)],
        out_shape=[jax.ShapeDtypeStruct((n, 2 * D_GROUP), _F32),
                   jax.ShapeDtypeStruct((n, D_GROUP), _BF16),
                   jax.ShapeDtypeStruct((n, D_GROUP), _F32),
                   jax.ShapeDtypeStruct((n, D_GROUP), _BF16),
                   jax.ShapeDtypeStruct((n, D_GROUP), _BF16),
                   jax.ShapeDtypeStruct((n, D_GROUP), _BF16),
                   jax.ShapeDtypeStruct((n, 2 * LANES), _F32)],
        compiler_params=pltpu.CompilerParams(dimension_semantics=("parallel",),
                                             vmem_limit_bytes=VMEM_LIMIT),
        name="in_proj",
    )(x2d, g, w, qg, kg)


def _mlstm_kernel(qk_ref, vm_ref, om_ref, gate_ref, cw_ref, cb_ref, gb_ref, og_ref,
                  y_ref, c_ref, m_ref, tail_ref, qkb_ref):
    ts = qk_ref.shape[0]
    L = MLSTM_CHUNK

    @pl.when(pl.program_id(1) == 0)
    def _():
        c_ref[...] = jnp.zeros_like(c_ref)
        m_ref[...] = jnp.zeros_like(m_ref)
        tail_ref[...] = jnp.zeros_like(tail_ref)

    u = qk_ref[...]
    tail = tail_ref[...]
    row8 = lax.broadcasted_iota(jnp.int32, tail.shape, 0)
    acc = cb_ref[...] + u * cw_ref[CONV_WIDTH - 1:CONV_WIDTH, :]
    for j in range(1, CONV_WIDTH):
        rolled = pltpu.roll(u, j, axis=0)
        head8 = jnp.where(row8 < j, pltpu.roll(tail, j, axis=0), rolled[0:8])
        shifted = jnp.concatenate([head8, rolled[8:]], axis=0)
        acc = acc + shifted * cw_ref[CONV_WIDTH - 1 - j:CONV_WIDTH - j, :]
    tail_ref[...] = u[ts - 8:ts]
    act = acc * (1.0 / (1.0 + jnp.exp(-acc)))
    col = lax.broadcasted_iota(jnp.int32, act.shape, 1)
    act = jnp.where(col < D_GROUP, act * (HEAD_DIM ** -0.5), act)
    qkb_ref[...] = act.astype(_BF16)

    r_i = lax.broadcasted_iota(jnp.int32, (L, L), 0)
    c_i = lax.broadcasted_iota(jnp.int32, (L, L), 1)
    tril = c_i <= r_i
    tril_bf = jnp.where(tril, 1.0, 0.0).astype(_BF16)
    lane128 = lax.broadcasted_iota(jnp.int32, (L, LANES), 1)
    lane256 = lax.broadcasted_iota(jnp.int32, (L, 2 * LANES), 1)
    first128 = lane128 < HEAD_DIM
    first256 = (lane256 // HEAD_DIM) % 2 == 0
    crow = lax.broadcasted_iota(jnp.int32, (LANES, 2 * LANES), 0)
    ccol = lax.broadcasted_iota(jnp.int32, (LANES, 2 * LANES), 1)
    c_first_row = crow < HEAD_DIM
    c_mask = (crow // HEAD_DIM) == ((ccol // HEAD_DIM) % 2)
    ones_bf = jnp.ones((L, LANES), _BF16)

    for c in range(ts // L):
        rows = slice(c * L, (c + 1) * L)
        gates = gate_ref[rows, :] + gb_ref[...]
        ic = gates[:, :LANES]
        fp = gates[:, LANES:]
        lf = jnp.minimum(fp, 0.0) - jnp.log(1.0 + jnp.exp(-jnp.abs(fp)))
        b = sum(_dot(tril_bf, part) for part in _split_bf16(lf, 3))
        r_t = (ic - b).T
        m_prev = m_ref[...]
        b_last = b[L - 1:L, :]
        g = b_last - b + ic
        m_new = jnp.maximum(b_last + m_prev, jnp.max(g, axis=0, keepdims=True))
        decay = jnp.exp(b_last + m_prev - m_new)
        w_s = jnp.exp(g - m_new)
        m_inter = b + m_prev
        m_ref[...] = m_new

        for p in range(N_PAIRS):
            lanes = slice(p * LANES, (p + 1) * LANES)
            q2 = qkb_ref[rows, p * LANES:(p + 1) * LANES]
            k2 = qkb_ref[rows, D_GROUP + p * LANES:D_GROUP + (p + 1) * LANES]
            vaug = jnp.concatenate([vm_ref[rows, lanes], ones_bf], axis=1)
            intra = jnp.zeros((L, 2 * LANES), _F32)
            s_int = []
            e_inv = []
            for e in range(2):
                hd = 2 * p + e
                d_log = jnp.where(tril, b[:, hd:hd + 1] + r_t[hd:hd + 1, :], _NEG)
                mi = m_inter[:, hd:hd + 1]
                m = jnp.maximum(mi, jnp.max(d_log, axis=1, keepdims=True))
                w_exp = jnp.exp(d_log - m)
                q_e = jnp.where(first128 if e == 0 else ~first128, q2, jnp.zeros_like(q2))
                w = (w_exp * _dot_nt(q_e, k2)).astype(_BF16)
                v_e = jnp.where(first256 if e == 0 else ~first256, vaug, jnp.zeros_like(vaug))
                intra = intra + _dot(w, v_e)
                s_int.append(jnp.exp(mi - m))
                e_inv.append(jnp.exp(-m))
            c_old = c_ref[p]
            inter = _dot(q2, c_old.astype(_BF16))
            tot = jnp.where(first256, s_int[0], s_int[1]) * inter + intra
            den = jnp.maximum(jnp.abs(tot[:, LANES:]), jnp.where(first128, e_inv[0], e_inv[1]))
            hh = tot[:, :LANES] / den
            y = _pair_rmsnorm(hh, og_ref[:, lanes])
            o = om_ref[rows, lanes]
            y_ref[rows, lanes] = (y * (1.0 / (1.0 + jnp.exp(-o)))).astype(_BF16)

            hd = 2 * p
            w2 = jnp.where(first128, w_s[:, hd:hd + 1], w_s[:, hd + 1:hd + 2])
            kw = (k2.astype(_F32) * w2).astype(_BF16)
            upd = _dot_tn(kw, vaug)
            dec = jnp.where(c_first_row, decay[:, hd:hd + 1], decay[:, hd + 1:hd + 2])
            c_ref[p] = dec * c_old + jnp.where(c_mask, upd, 0.0)


def _mlstm(qk, vm, om, gates, conv_w, conv_b, gate_b, out_g):
    bsz, s, _ = qk.shape
    ts = min(MLSTM_TILE, s)
    tok = lambda width: pl.BlockSpec((None, ts, width), lambda b, i: (b, i, 0))
    full = lambda shape: pl.BlockSpec(shape, lambda b, i: (0, 0))
    return pl.pallas_call(
        _mlstm_kernel,
        grid=(bsz, s // ts),
        in_specs=[tok(2 * D_GROUP), tok(D_GROUP), tok(D_GROUP), tok(2 * LANES),
                  full((CONV_WIDTH, 2 * D_GROUP)), full((1, 2 * D_GROUP)), full((1, 2 * LANES)),
                  full((1, D_GROUP))],
        out_specs=tok(D_GROUP),
        out_shape=jax.ShapeDtypeStruct((bsz, s, D_GROUP), _BF16),
        scratch_shapes=[pltpu.VMEM((N_PAIRS, LANES, 2 * LANES), _F32),
                        pltpu.VMEM((1, LANES), _F32),
                        pltpu.VMEM((8, 2 * D_GROUP), _F32),
                        pltpu.VMEM((ts, 2 * D_GROUP), _BF16)],
        compiler_params=pltpu.CompilerParams(dimension_semantics=("parallel", "arbitrary"),
                                             vmem_limit_bytes=VMEM_LIMIT),
        name="mlstm",
    )(qk, vm, om, gates, conv_w, conv_b, gate_b, out_g)


def _sb_kernel(q_ref, k_ref, v_ref, t_ref, og_ref, y_ref):
    tq = q_ref.shape[0]
    qi = pl.program_id(2)
    q2 = q_ref[...]
    lane = lax.broadcasted_iota(jnp.int32, q2.shape, 1)
    first = lane < HEAD_DIM
    zero = jnp.zeros_like(q2)
    q_heads = (jnp.where(first, q2, zero), jnp.where(first, zero, q2))
    t_mat = t_ref[...]
    r_i = lax.broadcasted_iota(jnp.int32, (tq, tq), 0)
    c_i = lax.broadcasted_iota(jnp.int32, (tq, tq), 1)
    causal = c_i < r_i

    def block(j, carry, masked):
        start = pl.multiple_of(j * tq, tq)
        kb = k_ref[pl.ds(start, tq), :]
        vb = v_ref[pl.ds(start, tq), :]
        out = []
        for e in range(2):
            acc, tot = carry[e]
            z = _dot_nt(q_heads[e], kb)
            sp = _softplus(z)
            if masked:
                sp = jnp.where(causal, sp, 0.0)
            hi, lo = _split_bf16(sp, 2)
            cum = _dot(hi, t_mat) + _dot(lo, t_mat)
            p = jnp.exp(z - cum)
            if masked:
                p = jnp.where(causal, p, 0.0)
            pv = _dot(p.astype(_BF16), vb)
            acc = acc + jnp.exp(-tot) * pv
            tot = tot + cum[:, 0:1]
            out.append((acc, tot))
        return tuple(out)

    init = tuple((jnp.zeros((tq, LANES), _F32), jnp.zeros((tq, 1), _F32)) for _ in range(2))
    carry = block(qi, init, True)
    carry = lax.fori_loop(0, qi, lambda i, c: block(qi - 1 - i, c, False), carry)
    y = jnp.where(first, carry[0][0], carry[1][0])
    y_ref[...] = _pair_rmsnorm(y, og_ref[...]).astype(_BF16)


def _sb_attn(qs, ks, vs, t_mat, out_g):
    bsz, s, _ = qs.shape
    tq = min(SB_TILE, s)
    return pl.pallas_call(
        _sb_kernel,
        grid=(bsz, N_PAIRS, s // tq),
        in_specs=[pl.BlockSpec((None, tq, LANES), lambda b, p, i: (b, i, p)),
                  pl.BlockSpec((None, s, LANES), lambda b, p, i: (b, 0, p)),
                  pl.BlockSpec((None, s, LANES), lambda b, p, i: (b, 0, p)),
                  pl.BlockSpec((tq, tq), lambda b, p, i: (0, 0)),
                  pl.BlockSpec((1, LANES), lambda b, p, i: (0, p))],
        out_specs=pl.BlockSpec((None, tq, LANES), lambda b, p, i: (b, i, p)),
        out_shape=jax.ShapeDtypeStruct((bsz, s, D_GROUP), _BF16),
        compiler_params=pltpu.CompilerParams(dimension_semantics=("parallel", "parallel", "arbitrary"),
                                             vmem_limit_bytes=VMEM_LIMIT),
        name="sb_attn",
    )(qs, ks, vs, t_mat, out_g)


def _out_mlp_kernel(x_ref, ym_ref, ys_ref, wo_ref, g_ref, wu_ref, wd_ref, o_ref, a_ref):
    x = x_ref[...] + _dot(ym_ref[...], wo_ref[0:D_GROUP, :]) + _dot(ys_ref[...], wo_ref[D_GROUP:, :])
    ms = jnp.mean(x * x, axis=-1, keepdims=True)
    h = (x * lax.rsqrt(ms + NORM_EPS) * g_ref[...]).astype(_BF16)
    for c in range(D_FF // FF_TILE):
        cols = slice(c * FF_TILE, (c + 1) * FF_TILE)
        up = jnp.maximum(_dot(h, wu_ref[:, cols]), 0.0)
        a_ref[:, cols] = (up * up).astype(_BF16)
    o_ref[...] = x + _dot(a_ref[...], wd_ref[...])


def _out_mlp(x2d, ym, ys, wo, g, wu, wd):
    n = x2d.shape[0]
    tn = TOK_TILE
    row = lambda width: pl.BlockSpec((tn, width), lambda i: (i, 0))
    full = lambda shape: pl.BlockSpec(shape, lambda i: (0, 0))
    return pl.pallas_call(
        _out_mlp_kernel,
        grid=(n // tn,),
        in_specs=[row(D_MODEL), row(D_GROUP), row(D_GROUP), full((D_MODEL, D_MODEL)),
                  full((1, D_MODEL)), full((D_MODEL, D_FF)), full((D_FF, D_MODEL))],
        out_specs=row(D_MODEL),
        out_shape=jax.ShapeDtypeStruct((n, D_MODEL), _F32),
        scratch_shapes=[pltpu.VMEM((tn, D_FF), _BF16)],
        compiler_params=pltpu.CompilerParams(dimension_semantics=("parallel",),
                                             vmem_limit_bytes=VMEM_LIMIT),
        name="out_mlp",
    )(x2d, ym, ys, wo, g, wu, wd)


def _pad_cols(a, width):
    return jnp.pad(a, ((0, 0), (0, width - a.shape[1])))


def kernel(x, attn_norm_g, w_in, conv_w, conv_b, b_igate, b_fgate, q_norm_g, k_norm_g, out_norm_g, w_out,
           mlp_norm_g, w_up, w_down):
    bsz, s, _ = x.shape
    n = bsz * s
    depth = w_in.shape[0]
    o_gate = 4 * D_GROUP
    o_sb = o_gate + 2 * N_HEADS
    tq = min(SB_TILE, s)
    t_mat = (jnp.arange(tq)[:, None] >= jnp.arange(tq)[None, :]).astype(_BF16)
    x2d = x.reshape(n, D_MODEL)
    for l in range(depth):
        w = jnp.concatenate([w_in[l][:, :o_gate], w_in[l][:, o_sb:],
                             _pad_cols(w_in[l][:, o_gate:o_gate + N_HEADS], LANES),
                             _pad_cols(w_in[l][:, o_gate + N_HEADS:o_sb], LANES)], axis=1).astype(_BF16)
        qg = jnp.tile(q_norm_g[l] * (HEAD_DIM ** -0.5), 2)[None, :]
        kg = jnp.tile(k_norm_g[l], 2)[None, :]
        gate_b = jnp.concatenate([_pad_cols(b_igate[l][None, :], LANES), _pad_cols(b_fgate[l][None, :], LANES)],
                                 axis=1)
        qk, vm, om, qs, ks, vs, gates = _in_proj(x2d, attn_norm_g[l][None, :], w, qg, kg)
        tok = lambda a: a.reshape(bsz, s, a.shape[-1])
        ym = _mlstm(tok(qk), tok(vm), tok(om), tok(gates), conv_w[l], conv_b[l][None, :], gate_b,
                    out_norm_g[l][None, :D_GROUP])
        ys = _sb_attn(tok(qs), tok(ks), tok(vs), t_mat, out_norm_g[l][None, D_GROUP:])
        x2d = _out_mlp(x2d, ym.reshape(n, D_GROUP), ys.reshape(n, D_GROUP), w_out[l].astype(_BF16),
                       mlp_norm_g[l][None, :], w_up[l].astype(_BF16), w_down[l].astype(_BF16))
    return x2d.reshape(bsz, s, D_MODEL)
```

```python
import functools

import jax
import jax.numpy as jnp
from jax import lax
from jax.experimental import pallas as pl
from jax.experimental.pallas import tpu as pltpu

D_MODEL = 1024
HEAD_DIM = 64
N_HEADS = 8
D_GROUP = N_HEADS * HEAD_DIM
N_PAIRS = N_HEADS // 2
D_FF = 4 * D_MODEL
CONV_WIDTH = 4
NORM_EPS = 1e-6
LANES = 128
D_PROJ = 7 * D_GROUP + 2 * LANES

TOK_TILE = 512
FF_TILE = 512
MLSTM_TILE = 512
MLSTM_CHUNK = 128
SB_TILE = 256
SB_SPLIT = 2
SB_DEAD_LOG2 = 160.0
LOG2E = 1.4426950408889634
VMEM_LIMIT = 56 * 1024 * 1024

_F32 = jnp.float32
_BF16 = jnp.bfloat16
_NEG = -1e30


def _dot(a, b):
    return jnp.dot(a, b, preferred_element_type=_F32)


def _dot_nt(a, b):
    return lax.dot_general(a, b, (((1,), (1,)), ((), ())), preferred_element_type=_F32)


def _dot_tn(a, b):
    return lax.dot_general(a, b, (((0,), (0,)), ((), ())), preferred_element_type=_F32)


def _split_bf16(a, pieces):
    parts = []
    r = a
    for i in range(pieces):
        p = r.astype(_BF16)
        parts.append(p)
        if i + 1 < pieces:
            r = r - p.astype(_F32)
    return parts


def _pair_rmsnorm(x, gain):
    lane = lax.broadcasted_iota(jnp.int32, x.shape, 1)
    first = lane < HEAD_DIM
    sq = x * x
    s_a = jnp.sum(jnp.where(first, sq, 0.0), axis=-1, keepdims=True)
    s_b = jnp.sum(jnp.where(first, 0.0, sq), axis=-1, keepdims=True)
    ms = jnp.where(first, s_a, s_b) * (1.0 / HEAD_DIM)
    return x * lax.rsqrt(ms + NORM_EPS) * gain


def _in_proj_kernel(x_ref, g_ref, w_ref, qg_ref, kg_ref,
                    qk_ref, vm_ref, om_ref, qs_ref, ks_ref, vs_ref, gate_ref):
    x = x_ref[...]
    ms = jnp.mean(x * x, axis=-1, keepdims=True)
    h = (x * lax.rsqrt(ms + NORM_EPS) * g_ref[...]).astype(_BF16)

    def proj(c):
        return _dot(h, w_ref[:, c * D_GROUP:(c + 1) * D_GROUP])

    qk_ref[:, 0:D_GROUP] = proj(0)
    qk_ref[:, D_GROUP:2 * D_GROUP] = proj(1)
    vm_ref[...] = proj(2).astype(_BF16)
    om_ref[...] = proj(3)
    for c, gain_ref, out_ref in ((4, qg_ref, qs_ref), (5, kg_ref, ks_ref)):
        r = proj(c)
        for p in range(N_PAIRS):
            sl = slice(p * LANES, (p + 1) * LANES)
            out_ref[:, sl] = _pair_rmsnorm(r[:, sl], gain_ref[...]).astype(_BF16)
    vs_ref[...] = proj(6).astype(_BF16)
    gate_ref[...] = _dot(h, w_ref[:, 7 * D_GROUP:D_PROJ])


def _in_proj(x2d, g, w, qg, kg):
    n = x2d.shape[0]
    tn = TOK_TILE
    row = lambda width: pl.BlockSpec((tn, width), lambda i: (i, 0))
    full = lambda shape: pl.BlockSpec(shape, lambda i: (0, 0))
    return pl.pallas_call(
        _in_proj_kernel,
        grid=(n // tn,),
        in_specs=[row(D_MODEL), full((1, D_MODEL)), full((D_MODEL, D_PROJ)),
                  full((1, LANES)), full((1, LANES))],
        out_specs=[row(2 * D_GROUP), row(D_GROUP), row(D_GROUP), row(D_GROUP), row(D_GROUP),
                   row(D_GROUP), row(2 * LANES)],
        out_shape=[jax.ShapeDtypeStruct((n, 2 * D_GROUP), _F32),
                   jax.ShapeDtypeStruct((n, D_GROUP), _BF16),
                   jax.ShapeDtypeStruct((n, D_GROUP), _F32),
                   jax.ShapeDtypeStruct((n, D_GROUP), _BF16),
                   jax.ShapeDtypeStruct((n, D_GROUP), _BF16),
                   jax.ShapeDtypeStruct((n, D_GROUP), _BF16),
                   jax.ShapeDtypeStruct((n, 2 * LANES), _F32)],
        compiler_params=pltpu.CompilerParams(dimension_semantics=("parallel",),
                                             vmem_limit_bytes=VMEM_LIMIT),
        name="in_proj",
    )(x2d, g, w, qg, kg)


def _mlstm_kernel(qk_ref, vm_ref, om_ref, gate_ref, cw_ref, cb_ref, gb_ref, og_ref,
                  y_ref, c_ref, m_ref, tail_ref, qkb_ref):
    ts = qk_ref.shape[0]
    L = MLSTM_CHUNK

    @pl.when(pl.program_id(1) == 0)
    def _():
        c_ref[...] = jnp.zeros_like(c_ref)
        m_ref[...] = jnp.zeros_like(m_ref)
        tail_ref[...] = jnp.zeros_like(tail_ref)

    u = qk_ref[...]
    tail = tail_ref[...]
    row8 = lax.broadcasted_iota(jnp.int32, tail.shape, 0)
    acc = cb_ref[...] + u * cw_ref[CONV_WIDTH - 1:CONV_WIDTH, :]
    for j in range(1, CONV_WIDTH):
        rolled = pltpu.roll(u, j, axis=0)
        head8 = jnp.where(row8 < j, pltpu.roll(tail, j, axis=0), rolled[0:8])
        shifted = jnp.concatenate([head8, rolled[8:]], axis=0)
        acc = acc + shifted * cw_ref[CONV_WIDTH - 1 - j:CONV_WIDTH - j, :]
    tail_ref[...] = u[ts - 8:ts]
    act = acc * (1.0 / (1.0 + jnp.exp(-acc)))
    col = lax.broadcasted_iota(jnp.int32, act.shape, 1)
    act = jnp.where(col < D_GROUP, act * (HEAD_DIM ** -0.5), act)
    qkb_ref[...] = act.astype(_BF16)

    r_i = lax.broadcasted_iota(jnp.int32, (L, L), 0)
    c_i = lax.broadcasted_iota(jnp.int32, (L, L), 1)
    tril = c_i <= r_i
    tril_bf = jnp.where(tril, 1.0, 0.0).astype(_BF16)
    lane128 = lax.broadcasted_iota(jnp.int32, (L, LANES), 1)
    lane256 = lax.broadcasted_iota(jnp.int32, (L, 2 * LANES), 1)
    first128 = lane128 < HEAD_DIM
    first256 = (lane256 // HEAD_DIM) % 2 == 0
    crow = lax.broadcasted_iota(jnp.int32, (LANES, 2 * LANES), 0)
    ccol = lax.broadcasted_iota(jnp.int32, (LANES, 2 * LANES), 1)
    c_first_row = crow < HEAD_DIM
    c_mask = (crow // HEAD_DIM) == ((ccol // HEAD_DIM) % 2)
    ones_bf = jnp.ones((L, LANES), _BF16)

    for c in range(ts // L):
        rows = slice(c * L, (c + 1) * L)
        gates = gate_ref[rows, :] + gb_ref[...]
        ic = gates[:, :LANES]
        fp = gates[:, LANES:]
        lf = jnp.minimum(fp, 0.0) - jnp.log(1.0 + jnp.exp(-jnp.abs(fp)))
        b = sum(_dot(tril_bf, part) for part in _split_bf16(lf, 3))
        r_t = (ic - b).T
        m_prev = m_ref[...]
        b_last = b[L - 1:L, :]
        g = b_last - b + ic
        m_new = jnp.maximum(b_last + m_prev, jnp.max(g, axis=0, keepdims=True))
        decay = jnp.exp(b_last + m_prev - m_new)
        w_s = jnp.exp(g - m_new)
        m_inter = b + m_prev
        m_ref[...] = m_new

        for p in range(N_PAIRS):
            lanes = slice(p * LANES, (p + 1) * LANES)
            q2 = qkb_ref[rows, p * LANES:(p + 1) * LANES]
            k2 = qkb_ref[rows, D_GROUP + p * LANES:D_GROUP + (p + 1) * LANES]
            vaug = jnp.concatenate([vm_ref[rows, lanes], ones_bf], axis=1)
            intra = jnp.zeros((L, 2 * LANES), _F32)
            s_int = []
            e_inv = []
            for e in range(2):
                hd = 2 * p + e
                d_log = jnp.where(tril, b[:, hd:hd + 1] + r_t[hd:hd + 1, :], _NEG)
                mi = m_inter[:, hd:hd + 1]
                m = jnp.maximum(mi, jnp.max(d_log, axis=1, keepdims=True))
                w_exp = jnp.exp(d_log - m)
                q_e = jnp.where(first128 if e == 0 else ~first128, q2, jnp.zeros_like(q2))
                w = (w_exp * _dot_nt(q_e, k2)).astype(_BF16)
                v_e = jnp.where(first256 if e == 0 else ~first256, vaug, jnp.zeros_like(vaug))
                intra = intra + _dot(w, v_e)
                s_int.append(jnp.exp(mi - m))
                e_inv.append(jnp.exp(-m))
            c_old = c_ref[p]
            inter = _dot(q2, c_old.astype(_BF16))
            tot = jnp.where(first256, s_int[0], s_int[1]) * inter + intra
            den = jnp.maximum(jnp.abs(tot[:, LANES:]), jnp.where(first128, e_inv[0], e_inv[1]))
            hh = tot[:, :LANES] / den
            y = _pair_rmsnorm(hh, og_ref[:, lanes])
            o = om_ref[rows, lanes]
            y_ref[rows, lanes] = (y * (1.0 / (1.0 + jnp.exp(-o)))).astype(_BF16)

            hd = 2 * p
            w2 = jnp.where(first128, w_s[:, hd:hd + 1], w_s[:, hd + 1:hd + 2])
            kw = (k2.astype(_F32) * w2).astype(_BF16)
            upd = _dot_tn(kw, vaug)
            dec = jnp.where(c_first_row, decay[:, hd:hd + 1], decay[:, hd + 1:hd + 2])
            c_ref[p] = dec * c_old + jnp.where(c_mask, upd, 0.0)


def _mlstm(qk, vm, om, gates, conv_w, conv_b, gate_b, out_g):
    bsz, s, _ = qk.shape
    ts = min(MLSTM_TILE, s)
    tok = lambda width: pl.BlockSpec((None, ts, width), lambda b, i: (b, i, 0))
    full = lambda shape: pl.BlockSpec(shape, lambda b, i: (0, 0))
    return pl.pallas_call(
        _mlstm_kernel,
        grid=(bsz, s // ts),
        in_specs=[tok(2 * D_GROUP), tok(D_GROUP), tok(D_GROUP), tok(2 * LANES),
                  full((CONV_WIDTH, 2 * D_GROUP)), full((1, 2 * D_GROUP)), full((1, 2 * LANES)),
                  full((1, D_GROUP))],
        out_specs=tok(D_GROUP),
        out_shape=jax.ShapeDtypeStruct((bsz, s, D_GROUP), _BF16),
        scratch_shapes=[pltpu.VMEM((N_PAIRS, LANES, 2 * LANES), _F32),
                        pltpu.VMEM((1, LANES), _F32),
                        pltpu.VMEM((8, 2 * D_GROUP), _F32),
                        pltpu.VMEM((ts, 2 * D_GROUP), _BF16)],
        compiler_params=pltpu.CompilerParams(dimension_semantics=("parallel", "arbitrary"),
                                             vmem_limit_bytes=VMEM_LIMIT),
        name="mlstm",
    )(qk, vm, om, gates, conv_w, conv_b, gate_b, out_g)


def _sb_kernel(q_ref, k_ref, v_ref, t_ref, og_ref, y_ref, acc_ref, tot_ref):
    tq = q_ref.shape[0]
    qi = pl.program_id(2)
    q2 = q_ref[...]
    lane = lax.broadcasted_iota(jnp.int32, q2.shape, 1)
    first = lane < HEAD_DIM
    zero = jnp.zeros_like(q2)
    q_heads = (jnp.where(first, q2, zero), jnp.where(first, zero, q2))
    r_i = lax.broadcasted_iota(jnp.int32, (tq, tq), 0)
    c_i = lax.broadcasted_iota(jnp.int32, (tq, tq), 1)
    causal = c_i < r_i

    def block_terms(q_e, j, mask):
        start = pl.multiple_of(j * tq, tq)
        kb = k_ref[pl.ds(start, tq), :]
        vb = v_ref[pl.ds(start, tq), :]
        z = _dot_nt(q_e, kb)
        neg_abs = lax.bitcast_convert_type(lax.bitcast_convert_type(z, jnp.int32) | jnp.int32(-2 ** 31), _F32)
        sp = jnp.maximum(z, 0.0) + jnp.log(1.0 + jnp.exp2(neg_abs)) * LOG2E
        if mask is not None:
            sp = jnp.where(mask, sp, 0.0)
        cum = _dot(jnp.concatenate(_split_bf16(sp, SB_SPLIT), axis=1), t_ref[...])
        d = z - cum
        if mask is not None:
            d = jnp.where(mask, d, _NEG)
        pv = _dot(jnp.exp2(d).astype(_BF16), vb)
        return pv, jnp.broadcast_to(cum[:, 0:1], (tq, LANES))

    has_prev = (qi > 0).astype(_F32)
    j_prev = jnp.maximum(qi - 1, 0)
    for e in range(2):
        pv0, rs0 = block_terms(q_heads[e], qi, causal)
        pv1, rs1 = block_terms(q_heads[e], j_prev, None)
        acc_ref[e] = pv0 + (has_prev * jnp.exp2(-rs0)) * pv1
        tot_ref[e] = rs0 + has_prev * rs1

    def alive():
        return (jnp.min(jnp.minimum(tot_ref[0], tot_ref[1])) < SB_DEAD_LOG2).astype(jnp.int32)

    def more(c):
        return jnp.logical_and(c[0] >= 0, c[1] > 0)

    def sweep(c):
        j = c[0]
        for e in range(2):
            pv, rs = block_terms(q_heads[e], j, None)
            tot = tot_ref[e]
            acc_ref[e] = acc_ref[e] + jnp.exp2(-tot) * pv
            tot_ref[e] = tot + rs
        return j - 1, alive()

    lax.while_loop(more, sweep, (qi - 2, alive()))
    y = jnp.where(first, acc_ref[0], acc_ref[1])
    y_ref[...] = _pair_rmsnorm(y, og_ref[...]).astype(_BF16)


def _sb_attn(qs, ks, vs, out_g):
    bsz, s, _ = qs.shape
    tq = min(SB_TILE, s)
    tri = (jnp.arange(tq)[:, None] >= jnp.arange(tq)[None, :]).astype(_BF16)
    t_mat = jnp.concatenate([tri] * SB_SPLIT, axis=0)
    return pl.pallas_call(
        _sb_kernel,
        grid=(bsz, N_PAIRS, s // tq),
        in_specs=[pl.BlockSpec((None, tq, LANES), lambda b, p, i: (b, i, p)),
                  pl.BlockSpec((None, s, LANES), lambda b, p, i: (b, 0, p)),
                  pl.BlockSpec((None, s, LANES), lambda b, p, i: (b, 0, p)),
                  pl.BlockSpec((SB_SPLIT * tq, tq), lambda b, p, i: (0, 0)),
                  pl.BlockSpec((1, LANES), lambda b, p, i: (0, p))],
        out_specs=pl.BlockSpec((None, tq, LANES), lambda b, p, i: (b, i, p)),
        out_shape=jax.ShapeDtypeStruct((bsz, s, D_GROUP), _BF16),
        scratch_shapes=[pltpu.VMEM((2, tq, LANES), _F32),
                        pltpu.VMEM((2, tq, LANES), _F32)],
        compiler_params=pltpu.CompilerParams(dimension_semantics=("parallel", "parallel", "arbitrary"),
                                             vmem_limit_bytes=VMEM_LIMIT),
        name="sb_attn",
    )(qs, ks, vs, t_mat, out_g)


def _out_mlp_kernel(x_ref, ym_ref, ys_ref, wo_ref, g_ref, wu_ref, wd_ref, o_ref, a_ref):
    x = x_ref[...] + _dot(ym_ref[...], wo_ref[0:D_GROUP, :]) + _dot(ys_ref[...], wo_ref[D_GROUP:, :])
    ms = jnp.mean(x * x, axis=-1, keepdims=True)
    h = (x * lax.rsqrt(ms + NORM_EPS) * g_ref[...]).astype(_BF16)
    for c in range(D_FF // FF_TILE):
        cols = slice(c * FF_TILE, (c + 1) * FF_TILE)
        up = jnp.maximum(_dot(h, wu_ref[:, cols]), 0.0)
        a_ref[:, cols] = (up * up).astype(_BF16)
    o_ref[...] = x + _dot(a_ref[...], wd_ref[...])


def _out_mlp(x2d, ym, ys, wo, g, wu, wd):
    n = x2d.shape[0]
    tn = TOK_TILE
    row = lambda width: pl.BlockSpec((tn, width), lambda i: (i, 0))
    full = lambda shape: pl.BlockSpec(shape, lambda i: (0, 0))
    return pl.pallas_call(
        _out_mlp_kernel,
        grid=(n // tn,),
        in_specs=[row(D_MODEL), row(D_GROUP), row(D_GROUP), full((D_MODEL, D_MODEL)),
                  full((1, D_MODEL)), full((D_MODEL, D_FF)), full((D_FF, D_MODEL))],
        out_specs=row(D_MODEL),
        out_shape=jax.ShapeDtypeStruct((n, D_MODEL), _F32),
        scratch_shapes=[pltpu.VMEM((tn, D_FF), _BF16)],
        compiler_params=pltpu.CompilerParams(dimension_semantics=("parallel",),
                                             vmem_limit_bytes=VMEM_LIMIT),
        name="out_mlp",
    )(x2d, ym, ys, wo, g, wu, wd)


def _pad_cols(a, width):
    return jnp.pad(a, ((0, 0), (0, width - a.shape[1])))


def kernel(x, attn_norm_g, w_in, conv_w, conv_b, b_igate, b_fgate, q_norm_g, k_norm_g, out_norm_g, w_out,
           mlp_norm_g, w_up, w_down):
    bsz, s, _ = x.shape
    n = bsz * s
    depth = w_in.shape[0]
    o_gate = 4 * D_GROUP
    o_sb = o_gate + 2 * N_HEADS
    x2d = x.reshape(n, D_MODEL)
    for l in range(depth):
        w = jnp.concatenate([w_in[l][:, :o_gate], w_in[l][:, o_sb:],
                             _pad_cols(w_in[l][:, o_gate:o_gate + N_HEADS], LANES),
                             _pad_cols(w_in[l][:, o_gate + N_HEADS:o_sb], LANES)], axis=1).astype(_BF16)
        qg = jnp.tile(q_norm_g[l] * (HEAD_DIM ** -0.5 * LOG2E), 2)[None, :]
        kg = jnp.tile(k_norm_g[l], 2)[None, :]
        gate_b = jnp.concatenate([_pad_cols(b_igate[l][None, :], LANES), _pad_cols(b_fgate[l][None, :], LANES)],
                                 axis=1)
        qk, vm, om, qs, ks, vs, gates = _in_proj(x2d, attn_norm_g[l][None, :], w, qg, kg)
        tok = lambda a: a.reshape(bsz, s, a.shape[-1])
        ym = _mlstm(tok(qk), tok(vm), tok(om), tok(gates), conv_w[l], conv_b[l][None, :], gate_b,
                    out_norm_g[l][None, :D_GROUP])
        ys = _sb_attn(tok(qs), tok(ks), tok(vs), out_norm_g[l][None, D_GROUP:])
        x2d = _out_mlp(x2d, ym.reshape(n, D_GROUP), ys.reshape(n, D_GROUP), w_out[l].astype(_BF16),
                       mlp_norm_g[l][None, :], w_up[l].astype(_BF16), w_down[l].astype(_BF16))
    return x2d.reshape(bsz, s, D_MODEL)
```

```python
import jax
import jax.numpy as jnp
from jax import lax
from jax.experimental import pallas as pl
from jax.experimental.pallas import tpu as pltpu

D_MODEL = 1024
HEAD_DIM = 64
N_HEADS = 8
D_GROUP = N_HEADS * HEAD_DIM
N_PAIRS = N_HEADS // 2
D_FF = 4 * D_MODEL
CONV_WIDTH = 4
NORM_EPS = 1e-6
LANES = 128
D_PROJ = 7 * D_GROUP + 2 * LANES

TOK_TILE = 512
FF_TILE = 512
MLSTM_TILE = 512
MLSTM_CHUNK = 128
GATE_REP = 8
SB_TILE = 256
SB_SPLIT = 2
SB_DEAD_LOG2 = 160.0
LOG2E = 1.4426950408889634
VMEM_LIMIT = 56 * 1024 * 1024

_F32 = jnp.float32
_BF16 = jnp.bfloat16
_NEG = -1e30


def _dot(a, b):
    return jnp.dot(a, b, preferred_element_type=_F32)


def _dot_nt(a, b):
    return lax.dot_general(a, b, (((1,), (1,)), ((), ())), preferred_element_type=_F32)


def _dot_tn(a, b):
    return lax.dot_general(a, b, (((0,), (0,)), ((), ())), preferred_element_type=_F32)


def _split_bf16(a, pieces):
    parts = []
    r = a
    for i in range(pieces):
        p = r.astype(_BF16)
        parts.append(p)
        if i + 1 < pieces:
            r = r - p.astype(_F32)
    return parts


def _pair_rmsnorm(x, gain):
    lane = lax.broadcasted_iota(jnp.int32, x.shape, 1)
    first = lane < HEAD_DIM
    sq = x * x
    s_a = jnp.sum(jnp.where(first, sq, 0.0), axis=-1, keepdims=True)
    s_b = jnp.sum(jnp.where(first, 0.0, sq), axis=-1, keepdims=True)
    ms = jnp.where(first, s_a, s_b) * (1.0 / HEAD_DIM)
    return x * lax.rsqrt(ms + NORM_EPS) * gain


def _in_proj_kernel(x_ref, g_ref, w_ref, qg_ref, kg_ref,
                    qk_ref, vm_ref, om_ref, qs_ref, ks_ref, vs_ref, gate_ref):
    x = x_ref[...]
    ms = jnp.mean(x * x, axis=-1, keepdims=True)
    h = (x * lax.rsqrt(ms + NORM_EPS) * g_ref[...]).astype(_BF16)

    def proj(c):
        return _dot(h, w_ref[:, c * D_GROUP:(c + 1) * D_GROUP])

    qk_ref[:, 0:D_GROUP] = proj(0)
    qk_ref[:, D_GROUP:2 * D_GROUP] = proj(1)
    vm_ref[...] = proj(2).astype(_BF16)
    om_ref[...] = proj(3)
    for c, gain_ref, out_ref in ((4, qg_ref, qs_ref), (5, kg_ref, ks_ref)):
        r = proj(c)
        for p in range(N_PAIRS):
            sl = slice(p * LANES, (p + 1) * LANES)
            out_ref[:, sl] = _pair_rmsnorm(r[:, sl], gain_ref[...]).astype(_BF16)
    vs_ref[...] = proj(6).astype(_BF16)
    gate_ref[...] = _dot(h, w_ref[:, 7 * D_GROUP:D_PROJ])


def _in_proj(x2d, g, w, qg, kg):
    n = x2d.shape[0]
    tn = TOK_TILE
    row = lambda width: pl.BlockSpec((tn, width), lambda i: (i, 0))
    full = lambda shape: pl.BlockSpec(shape, lambda i: (0, 0))
    return pl.pallas_call(
        _in_proj_kernel,
        grid=(n // tn,),
        in_specs=[row(D_MODEL), full((1, D_MODEL)), full((D_MODEL, D_PROJ)),
                  full((1, LANES)), full((1, LANES))],
        out_specs=[row(2 * D_GROUP), row(D_GROUP), row(D_GROUP), row(D_GROUP), row(D_GROUP),
                   row(D_GROUP), row(2 * LANES)],
        out_shape=[jax.ShapeDtypeStruct((n, 2 * D_GROUP), _F32),
                   jax.ShapeDtypeStruct((n, D_GROUP), _BF16),
                   jax.ShapeDtypeStruct((n, D_GROUP), _F32),
                   jax.ShapeDtypeStruct((n, D_GROUP), _BF16),
                   jax.ShapeDtypeStruct((n, D_GROUP), _BF16),
                   jax.ShapeDtypeStruct((n, D_GROUP), _BF16),
                   jax.ShapeDtypeStruct((n, 2 * LANES), _F32)],
        compiler_params=pltpu.CompilerParams(dimension_semantics=("parallel",),
                                             vmem_limit_bytes=VMEM_LIMIT),
        name="in_proj",
    )(x2d, g, w, qg, kg)


def _mlstm_kernel(qk_ref, vm_ref, om_ref, gate_ref, cw_ref, cb_ref, gb_ref, og_ref, esel_ref, bsel_ref,
                  y_ref, c_ref, m_ref, ubuf_ref, qkb_ref):
    ts = qk_ref.shape[0]
    L = MLSTM_CHUNK

    @pl.when(pl.program_id(1) == 0)
    def _():
        c_ref[...] = jnp.zeros_like(c_ref)
        m_ref[...] = jnp.zeros_like(m_ref)
        ubuf_ref[0:8, :] = jnp.zeros((8, 2 * D_GROUP), _F32)

    ubuf_ref[8:, :] = qk_ref[...]
    acc = cb_ref[...] + ubuf_ref[8:, :] * cw_ref[CONV_WIDTH - 1:CONV_WIDTH, :]
    for j in range(1, CONV_WIDTH):
        acc = acc + ubuf_ref[8 - j:8 - j + ts, :] * cw_ref[CONV_WIDTH - 1 - j:CONV_WIDTH - j, :]
    ubuf_ref[0:8, :] = ubuf_ref[ts:ts + 8, :]
    act = acc * (1.0 / (1.0 + jnp.exp(-acc)))
    qkb_ref[:, :D_GROUP] = (act[:, :D_GROUP] * (HEAD_DIM ** -0.5)).astype(_BF16)
    qkb_ref[:, D_GROUP:] = act[:, D_GROUP:].astype(_BF16)

    r_i = lax.broadcasted_iota(jnp.int32, (L, L), 0)
    c_i = lax.broadcasted_iota(jnp.int32, (L, L), 1)
    tril_bf = jnp.where(c_i <= r_i, 1.0, 0.0).astype(_BF16)
    tril3 = jnp.concatenate([tril_bf] * 3, axis=1)
    r2 = lax.broadcasted_iota(jnp.int32, (L, 2 * L), 0)
    c2 = lax.broadcasted_iota(jnp.int32, (L, 2 * L), 1)
    tril2 = (c2 % L) <= r2
    row128 = lax.broadcasted_iota(jnp.int32, (L, LANES), 0)
    lane128 = lax.broadcasted_iota(jnp.int32, (L, LANES), 1)
    lane256 = lax.broadcasted_iota(jnp.int32, (L, 2 * LANES), 1)
    first128 = lane128 < HEAD_DIM
    first256 = (lane256 // HEAD_DIM) % 2 == 0
    crow = lax.broadcasted_iota(jnp.int32, (LANES, 2 * LANES), 0)
    ccol = lax.broadcasted_iota(jnp.int32, (LANES, 2 * LANES), 1)
    c_mask = (crow // HEAD_DIM) == ((ccol // HEAD_DIM) % 2)
    ones_bf = jnp.ones((L, LANES), _BF16)
    rep = lane128 // N_HEADS

    def by_copy(pieces, first_copy, fill):
        out = jnp.full((L, LANES), fill, pieces[0].dtype)
        for k, piece in enumerate(pieces):
            out = jnp.where(rep == first_copy + k, piece, out)
        return out

    for c in range(ts // L):
        rows = slice(c * L, (c + 1) * L)
        gates = gate_ref[rows, :] + gb_ref[...]
        ic = gates[:, :LANES]
        fp = gates[:, LANES:]
        lf = jnp.minimum(fp, 0.0) - jnp.log(1.0 + jnp.exp(-jnp.abs(fp)))
        b = _dot(tril3, jnp.concatenate(_split_bf16(lf, 3), axis=0))
        r = ic - b
        run = r
        k = 1
        while k < L:
            run = jnp.maximum(run, jnp.where(row128 >= k, pltpu.roll(run, k, axis=0), _NEG))
            k *= 2
        m_prev = m_ref[...]
        big_m = jnp.maximum(run, m_prev)
        m = b + big_m
        s_inter = jnp.exp(m_prev - big_m)
        w_s = jnp.exp(r - big_m[L - 1:L, :])
        m_ref[...] = m[L - 1:L, :]

        lhs_e = by_copy(_split_bf16(-big_m, 3) + [ones_bf] * 3, 0, 0.0)
        r_pieces = by_copy([p.astype(_F32) for p in _split_bf16(r, 3)], 3, 0.0)
        r_t = jnp.concatenate([r_pieces.T.astype(_BF16)] * N_HEADS, axis=1)
        esel = esel_ref[...]
        rhs_e = jnp.where(esel == 1, jnp.ones_like(r_t), jnp.where(esel == 2, r_t, jnp.zeros_like(r_t)))
        e_all = _dot(lhs_e, rhs_e)
        lhs_b = by_copy(_split_bf16(s_inter, 3) + _split_bf16(w_s, 2) + _split_bf16(m, 3), 0, 0.0)
        bc = _dot(lhs_b, bsel_ref[...])

        for p in range(N_PAIRS):
            lanes = slice(p * LANES, (p + 1) * LANES)
            s2 = bc[:, 2 * p * LANES:2 * (p + 1) * LANES]
            w2 = bc[:, (2 * N_PAIRS + p) * LANES:(2 * N_PAIRS + p + 1) * LANES]
            m2 = bc[:, (3 * N_PAIRS + p) * LANES:(3 * N_PAIRS + p + 1) * LANES]
            q2 = qkb_ref[rows, p * LANES:(p + 1) * LANES]
            k2 = qkb_ref[rows, D_GROUP + p * LANES:D_GROUP + (p + 1) * LANES]
            vaug = jnp.concatenate([vm_ref[rows, lanes], ones_bf], axis=1)
            zk = jnp.zeros_like(k2)
            zv = jnp.zeros_like(vaug)
            k_cat = jnp.concatenate([jnp.where(first128, k2, zk), jnp.where(first128, zk, k2)], axis=0)
            v_cat = jnp.concatenate([jnp.where(first256, vaug, zv), jnp.where(first256, zv, vaug)], axis=0)
            e2 = e_all[:, 2 * p * L:2 * (p + 1) * L]
            w = (jnp.exp(jnp.where(tril2, e2, _NEG)) * _dot_nt(q2, k_cat)).astype(_BF16)
            c_old = c_ref[p]
            tot = s2 * _dot(q2, c_old.astype(_BF16)) + _dot(w, v_cat)
            den = jnp.maximum(jnp.abs(tot[:, LANES:]), jnp.exp(-m2))
            y = _pair_rmsnorm(tot[:, :LANES] / den, og_ref[:, lanes])
            o = om_ref[rows, lanes]
            y_ref[rows, lanes] = (y * (1.0 / (1.0 + jnp.exp(-o)))).astype(_BF16)
            kw = (k2.astype(_F32) * w2).astype(_BF16)
            c_ref[p] = s2[L - 1:L, :] * c_old + jnp.where(c_mask, _dot_tn(kw, vaug), 0.0)


def _mlstm_selectors():
    L = MLSTM_CHUNK
    row = jnp.arange(LANES)[:, None]
    head, rep = row % N_HEADS, row // N_HEADS
    col_head = jnp.arange(N_HEADS * L)[None, :] // L
    esel = jnp.where(head == col_head, jnp.where(rep < 3, 1, jnp.where(rep < 6, 2, 0)), 0).astype(_BF16)
    col = jnp.arange(4 * N_PAIRS * LANES)[None, :]
    grp = col // LANES
    pair = jnp.where(grp < 2 * N_PAIRS, grp // 2, (grp - 2 * N_PAIRS) % N_PAIRS)
    col_head = 2 * pair + (col % LANES) // HEAD_DIM
    src = jnp.where(grp < 2 * N_PAIRS, 0, jnp.where(grp < 3 * N_PAIRS, 1, 2))
    row_src = jnp.where(rep < 3, 0, jnp.where(rep < 5, 1, 2))
    bsel = ((head == col_head) & (row_src == src)).astype(_BF16)
    return esel, bsel


def _mlstm(qk, vm, om, gates, conv_w, conv_b, gate_b, out_g, esel, bsel):
    bsz, s, _ = qk.shape
    ts = min(MLSTM_TILE, s)
    tok = lambda width: pl.BlockSpec((None, ts, width), lambda b, i: (b, i, 0))
    full = lambda shape: pl.BlockSpec(shape, lambda b, i: (0, 0))
    return pl.pallas_call(
        _mlstm_kernel,
        grid=(bsz, s // ts),
        in_specs=[tok(2 * D_GROUP), tok(D_GROUP), tok(D_GROUP), tok(2 * LANES),
                  full((CONV_WIDTH, 2 * D_GROUP)), full((1, 2 * D_GROUP)), full((1, 2 * LANES)),
                  full((1, D_GROUP)), full(esel.shape), full(bsel.shape)],
        out_specs=tok(D_GROUP),
        out_shape=jax.ShapeDtypeStruct((bsz, s, D_GROUP), _BF16),
        scratch_shapes=[pltpu.VMEM((N_PAIRS, LANES, 2 * LANES), _F32),
                        pltpu.VMEM((1, LANES), _F32),
                        pltpu.VMEM((ts + 8, 2 * D_GROUP), _F32),
                        pltpu.VMEM((ts, 2 * D_GROUP), _BF16)],
        compiler_params=pltpu.CompilerParams(dimension_semantics=("parallel", "arbitrary"),
                                             vmem_limit_bytes=VMEM_LIMIT),
        name="mlstm",
    )(qk, vm, om, gates, conv_w, conv_b, gate_b, out_g, esel, bsel)


def _sb_kernel(q_ref, k_ref, v_ref, t_ref, og_ref, y_ref, acc_ref, tot_ref):
    tq = q_ref.shape[0]
    qi = pl.program_id(2)
    q2 = q_ref[...]
    lane = lax.broadcasted_iota(jnp.int32, q2.shape, 1)
    first = lane < HEAD_DIM
    zero = jnp.zeros_like(q2)
    q_heads = (jnp.where(first, q2, zero), jnp.where(first, zero, q2))
    r_i = lax.broadcasted_iota(jnp.int32, (tq, tq), 0)
    c_i = lax.broadcasted_iota(jnp.int32, (tq, tq), 1)
    causal = c_i < r_i

    def block_terms(q_e, j, mask):
        start = pl.multiple_of(j * tq, tq)
        kb = k_ref[pl.ds(start, tq), :]
        vb = v_ref[pl.ds(start, tq), :]
        z = _dot_nt(q_e, kb)
        sp = jnp.maximum(z, 0.0) + jnp.log(1.0 + jnp.exp2(-jnp.abs(z))) * LOG2E
        if mask is not None:
            sp = jnp.where(mask, sp, 0.0)
        cum = _dot(jnp.concatenate(_split_bf16(sp, SB_SPLIT), axis=1), t_ref[...])
        d = z - cum
        if mask is not None:
            d = jnp.where(mask, d, _NEG)
        pv = _dot(jnp.exp2(d).astype(_BF16), vb)
        return pv, jnp.broadcast_to(cum[:, 0:1], (tq, LANES))

    has_prev = (qi > 0).astype(_F32)
    j_prev = jnp.maximum(qi - 1, 0)
    for e in range(2):
        pv0, rs0 = block_terms(q_heads[e], qi, causal)
        pv1, rs1 = block_terms(q_heads[e], j_prev, None)
        acc_ref[e] = pv0 + (has_prev * jnp.exp2(-rs0)) * pv1
        tot_ref[e] = rs0 + has_prev * rs1

    def alive():
        return (jnp.min(jnp.minimum(tot_ref[0], tot_ref[1])) < SB_DEAD_LOG2).astype(jnp.int32)

    def more(c):
        return jnp.logical_and(c[0] >= 0, c[1] > 0)

    def sweep(c):
        j = c[0]
        for e in range(2):
            pv, rs = block_terms(q_heads[e], j, None)
            tot = tot_ref[e]
            acc_ref[e] = acc_ref[e] + jnp.exp2(-tot) * pv
            tot_ref[e] = tot + rs
        return j - 1, alive()

    lax.while_loop(more, sweep, (qi - 2, alive()))
    y = jnp.where(first, acc_ref[0], acc_ref[1])
    y_ref[...] = _pair_rmsnorm(y, og_ref[...]).astype(_BF16)


def _sb_attn(qs, ks, vs, out_g):
    bsz, s, _ = qs.shape
    tq = min(SB_TILE, s)
    tri = (jnp.arange(tq)[:, None] >= jnp.arange(tq)[None, :]).astype(_BF16)
    t_mat = jnp.concatenate([tri] * SB_SPLIT, axis=0)
    return pl.pallas_call(
        _sb_kernel,
        grid=(bsz, N_PAIRS, s // tq),
        in_specs=[pl.BlockSpec((None, tq, LANES), lambda b, p, i: (b, i, p)),
                  pl.BlockSpec((None, s, LANES), lambda b, p, i: (b, 0, p)),
                  pl.BlockSpec((None, s, LANES), lambda b, p, i: (b, 0, p)),
                  pl.BlockSpec((SB_SPLIT * tq, tq), lambda b, p, i: (0, 0)),
                  pl.BlockSpec((1, LANES), lambda b, p, i: (0, p))],
        out_specs=pl.BlockSpec((None, tq, LANES), lambda b, p, i: (b, i, p)),
        out_shape=jax.ShapeDtypeStruct((bsz, s, D_GROUP), _BF16),
        scratch_shapes=[pltpu.VMEM((2, tq, LANES), _F32),
                        pltpu.VMEM((2, tq, LANES), _F32)],
        compiler_params=pltpu.CompilerParams(dimension_semantics=("parallel", "parallel", "arbitrary"),
                                             vmem_limit_bytes=VMEM_LIMIT),
        name="sb_attn",
    )(qs, ks, vs, t_mat, out_g)


def _out_mlp_kernel(x_ref, ym_ref, ys_ref, wo_ref, g_ref, wu_ref, wd_ref, o_ref, a_ref):
    x = x_ref[...] + _dot(ym_ref[...], wo_ref[0:D_GROUP, :]) + _dot(ys_ref[...], wo_ref[D_GROUP:, :])
    ms = jnp.mean(x * x, axis=-1, keepdims=True)
    h = (x * lax.rsqrt(ms + NORM_EPS) * g_ref[...]).astype(_BF16)
    for c in range(D_FF // FF_TILE):
        cols = slice(c * FF_TILE, (c + 1) * FF_TILE)
        up = jnp.maximum(_dot(h, wu_ref[:, cols]), 0.0)
        a_ref[:, cols] = (up * up).astype(_BF16)
    o_ref[...] = x + _dot(a_ref[...], wd_ref[...])


def _out_mlp(x2d, ym, ys, wo, g, wu, wd):
    n = x2d.shape[0]
    tn = TOK_TILE
    row = lambda width: pl.BlockSpec((tn, width), lambda i: (i, 0))
    full = lambda shape: pl.BlockSpec(shape, lambda i: (0, 0))
    return pl.pallas_call(
        _out_mlp_kernel,
        grid=(n // tn,),
        in_specs=[row(D_MODEL), row(D_GROUP), row(D_GROUP), full((D_MODEL, D_MODEL)),
                  full((1, D_MODEL)), full((D_MODEL, D_FF)), full((D_FF, D_MODEL))],
        out_specs=row(D_MODEL),
        out_shape=jax.ShapeDtypeStruct((n, D_MODEL), _F32),
        scratch_shapes=[pltpu.VMEM((tn, D_FF), _BF16)],
        compiler_params=pltpu.CompilerParams(dimension_semantics=("parallel",),
                                             vmem_limit_bytes=VMEM_LIMIT),
        name="out_mlp",
    )(x2d, ym, ys, wo, g, wu, wd)


def _pad_cols(a, width):
    return jnp.pad(a, ((0, 0), (0, width - a.shape[1])))


def kernel(x, attn_norm_g, w_in, conv_w, conv_b, b_igate, b_fgate, q_norm_g, k_norm_g, out_norm_g, w_out,
           mlp_norm_g, w_up, w_down):
    bsz, s, _ = x.shape
    n = bsz * s
    depth = w_in.shape[0]
    o_gate = 4 * D_GROUP
    o_sb = o_gate + 2 * N_HEADS
    x2d = x.reshape(n, D_MODEL)
    esel, bsel = _mlstm_selectors()
    for l in range(depth):
        w = jnp.concatenate([w_in[l][:, :o_gate], w_in[l][:, o_sb:],
                             _pad_cols(jnp.tile(w_in[l][:, o_gate:o_gate + N_HEADS], (1, GATE_REP)), LANES),
                             _pad_cols(jnp.tile(w_in[l][:, o_gate + N_HEADS:o_sb], (1, GATE_REP)), LANES)],
                            axis=1).astype(_BF16)
        qg = jnp.tile(q_norm_g[l] * (HEAD_DIM ** -0.5 * LOG2E), 2)[None, :]
        kg = jnp.tile(k_norm_g[l], 2)[None, :]
        gate_b = jnp.concatenate([_pad_cols(jnp.tile(b_igate[l], GATE_REP)[None, :], LANES),
                                  _pad_cols(jnp.tile(b_fgate[l], GATE_REP)[None, :], LANES)], axis=1)
        qk, vm, om, qs, ks, vs, gates = _in_proj(x2d, attn_norm_g[l][None, :], w, qg, kg)
        tok = lambda a: a.reshape(bsz, s, a.shape[-1])
        ym = _mlstm(tok(qk), tok(vm), tok(om), tok(gates), conv_w[l], conv_b[l][None, :], gate_b,
                    out_norm_g[l][None, :D_GROUP], esel, bsel)
        ys = _sb_attn(tok(qs), tok(ks), tok(vs), out_norm_g[l][None, D_GROUP:])
        x2d = _out_mlp(x2d, ym.reshape(n, D_GROUP), ys.reshape(n, D_GROUP), w_out[l].astype(_BF16),
                       mlp_norm_g[l][None, :], w_up[l].astype(_BF16), w_down[l].astype(_BF16))
    return x2d.reshape(bsz, s, D_MODEL)
```

```python
import jax
import jax.numpy as jnp
from jax import lax
from jax.experimental import pallas as pl
from jax.experimental.pallas import tpu as pltpu

D_MODEL = 1024
HEAD_DIM = 64
N_HEADS = 8
D_GROUP = N_HEADS * HEAD_DIM
N_PAIRS = N_HEADS // 2
D_FF = 4 * D_MODEL
CONV_WIDTH = 4
NORM_EPS = 1e-6
LANES = 128
D_PROJ = 7 * D_GROUP + 2 * LANES

TOK_TILE = 512
FF_TILE = 512
MLSTM_TILE = 512
MLSTM_CHUNK = 128
GATE_REP = 8
SB_TILE = 256
SB_GROUP = 4
SB_SPLIT = 2
SB_DEAD_LOG2 = 160.0
LOG2E = 1.4426950408889634
VMEM_LIMIT = 56 * 1024 * 1024

_F32 = jnp.float32
_BF16 = jnp.bfloat16
_NEG = -1e30


def _dot(a, b):
    return jnp.dot(a, b, preferred_element_type=_F32)


def _dot_nt(a, b):
    return lax.dot_general(a, b, (((1,), (1,)), ((), ())), preferred_element_type=_F32)


def _dot_tn(a, b):
    return lax.dot_general(a, b, (((0,), (0,)), ((), ())), preferred_element_type=_F32)


def _split_bf16(a, pieces):
    parts = []
    r = a
    for i in range(pieces):
        p = r.astype(_BF16)
        parts.append(p)
        if i + 1 < pieces:
            r = r - p.astype(_F32)
    return parts


def _pair_rmsnorm(x, gain):
    lane = lax.broadcasted_iota(jnp.int32, x.shape, 1)
    first = lane < HEAD_DIM
    sq = x * x
    s_a = jnp.sum(jnp.where(first, sq, 0.0), axis=-1, keepdims=True)
    s_b = jnp.sum(jnp.where(first, 0.0, sq), axis=-1, keepdims=True)
    ms = jnp.where(first, s_a, s_b) * (1.0 / HEAD_DIM)
    return x * lax.rsqrt(ms + NORM_EPS) * gain


def _in_proj_kernel(x_ref, g_ref, w_ref, qg_ref, kg_ref,
                    qk_ref, vm_ref, om_ref, qs_ref, ks_ref, vs_ref, gate_ref):
    x = x_ref[...]
    ms = jnp.mean(x * x, axis=-1, keepdims=True)
    h = (x * lax.rsqrt(ms + NORM_EPS) * g_ref[...]).astype(_BF16)

    def proj(c):
        return _dot(h, w_ref[:, c * D_GROUP:(c + 1) * D_GROUP])

    qk_ref[:, 0:D_GROUP] = proj(0)
    qk_ref[:, D_GROUP:2 * D_GROUP] = proj(1)
    vm_ref[...] = proj(2).astype(_BF16)
    om_ref[...] = proj(3)
    for c, gain_ref, out_ref in ((4, qg_ref, qs_ref), (5, kg_ref, ks_ref)):
        r = proj(c)
        for p in range(N_PAIRS):
            sl = slice(p * LANES, (p + 1) * LANES)
            out_ref[:, sl] = _pair_rmsnorm(r[:, sl], gain_ref[...]).astype(_BF16)
    vs_ref[...] = proj(6).astype(_BF16)
    gate_ref[...] = _dot(h, w_ref[:, 7 * D_GROUP:D_PROJ])


def _in_proj(x2d, g, w, qg, kg):
    n = x2d.shape[0]
    tn = TOK_TILE
    row = lambda width: pl.BlockSpec((tn, width), lambda i: (i, 0))
    full = lambda shape: pl.BlockSpec(shape, lambda i: (0, 0))
    return pl.pallas_call(
        _in_proj_kernel,
        grid=(n // tn,),
        in_specs=[row(D_MODEL), full((1, D_MODEL)), full((D_MODEL, D_PROJ)),
                  full((1, LANES)), full((1, LANES))],
        out_specs=[row(2 * D_GROUP), row(D_GROUP), row(D_GROUP), row(D_GROUP), row(D_GROUP),
                   row(D_GROUP), row(2 * LANES)],
        out_shape=[jax.ShapeDtypeStruct((n, 2 * D_GROUP), _F32),
                   jax.ShapeDtypeStruct((n, D_GROUP), _BF16),
                   jax.ShapeDtypeStruct((n, D_GROUP), _F32),
                   jax.ShapeDtypeStruct((n, D_GROUP), _BF16),
                   jax.ShapeDtypeStruct((n, D_GROUP), _BF16),
                   jax.ShapeDtypeStruct((n, D_GROUP), _BF16),
                   jax.ShapeDtypeStruct((n, 2 * LANES), _F32)],
        compiler_params=pltpu.CompilerParams(dimension_semantics=("parallel",),
                                             vmem_limit_bytes=VMEM_LIMIT),
        name="in_proj",
    )(x2d, g, w, qg, kg)


def _mlstm_kernel(qk_ref, vm_ref, om_ref, gate_ref, cw_ref, cb_ref, gb_ref, og_ref, esel_ref, bsel_ref,
                  y_ref, c_ref, m_ref, ubuf_ref, qkb_ref):
    ts = qk_ref.shape[0]
    L = MLSTM_CHUNK

    @pl.when(pl.program_id(1) == 0)
    def _():
        c_ref[...] = jnp.zeros_like(c_ref)
        m_ref[...] = jnp.zeros_like(m_ref)
        ubuf_ref[0:8, :] = jnp.zeros((8, 2 * D_GROUP), _F32)

    ubuf_ref[8:, :] = qk_ref[...]
    acc = cb_ref[...] + ubuf_ref[8:, :] * cw_ref[CONV_WIDTH - 1:CONV_WIDTH, :]
    for j in range(1, CONV_WIDTH):
        acc = acc + ubuf_ref[8 - j:8 - j + ts, :] * cw_ref[CONV_WIDTH - 1 - j:CONV_WIDTH - j, :]
    ubuf_ref[0:8, :] = ubuf_ref[ts:ts + 8, :]
    act = acc * (1.0 / (1.0 + jnp.exp(-acc)))
    qkb_ref[:, :D_GROUP] = (act[:, :D_GROUP] * (HEAD_DIM ** -0.5)).astype(_BF16)
    qkb_ref[:, D_GROUP:] = act[:, D_GROUP:].astype(_BF16)

    r_i = lax.broadcasted_iota(jnp.int32, (L, L), 0)
    c_i = lax.broadcasted_iota(jnp.int32, (L, L), 1)
    tril_bf = jnp.where(c_i <= r_i, 1.0, 0.0).astype(_BF16)
    tril3 = jnp.concatenate([tril_bf] * 3, axis=1)
    r2 = lax.broadcasted_iota(jnp.int32, (L, 2 * L), 0)
    c2 = lax.broadcasted_iota(jnp.int32, (L, 2 * L), 1)
    tril2 = (c2 % L) <= r2
    row128 = lax.broadcasted_iota(jnp.int32, (L, LANES), 0)
    lane128 = lax.broadcasted_iota(jnp.int32, (L, LANES), 1)
    lane256 = lax.broadcasted_iota(jnp.int32, (L, 2 * LANES), 1)
    first128 = lane128 < HEAD_DIM
    first256 = (lane256 // HEAD_DIM) % 2 == 0
    crow = lax.broadcasted_iota(jnp.int32, (LANES, 2 * LANES), 0)
    ccol = lax.broadcasted_iota(jnp.int32, (LANES, 2 * LANES), 1)
    c_mask = (crow // HEAD_DIM) == ((ccol // HEAD_DIM) % 2)
    ones_bf = jnp.ones((L, LANES), _BF16)
    rep = lane128 // N_HEADS

    def by_copy(pieces, first_copy, fill):
        out = jnp.full((L, LANES), fill, pieces[0].dtype)
        for k, piece in enumerate(pieces):
            out = jnp.where(rep == first_copy + k, piece, out)
        return out

    for c in range(ts // L):
        rows = slice(c * L, (c + 1) * L)
        gates = gate_ref[rows, :] + gb_ref[...]
        ic = gates[:, :LANES]
        fp = gates[:, LANES:]
        lf = jnp.minimum(fp, 0.0) - jnp.log(1.0 + jnp.exp(-jnp.abs(fp)))
        b = _dot(tril3, jnp.concatenate(_split_bf16(lf, 3), axis=0))
        r = ic - b
        run = r
        k = 1
        while k < L:
            run = jnp.maximum(run, jnp.where(row128 >= k, pltpu.roll(run, k, axis=0), _NEG))
            k *= 2
        m_prev = m_ref[...]
        big_m = jnp.maximum(run, m_prev)
        m = b + big_m
        s_inter = jnp.exp(m_prev - big_m)
        w_s = jnp.exp(r - big_m[L - 1:L, :])
        m_ref[...] = m[L - 1:L, :]

        lhs_e = by_copy(_split_bf16(-big_m, 3) + [ones_bf] * 3, 0, 0.0)
        r_pieces = by_copy([p.astype(_F32) for p in _split_bf16(r, 3)], 3, 0.0)
        r_t = jnp.concatenate([r_pieces.T.astype(_BF16)] * N_HEADS, axis=1)
        esel = esel_ref[...]
        rhs_e = jnp.where(esel == 1, jnp.ones_like(r_t), jnp.where(esel == 2, r_t, jnp.zeros_like(r_t)))
        e_all = _dot(lhs_e, rhs_e)
        lhs_b = by_copy(_split_bf16(s_inter, 3) + _split_bf16(w_s, 2) + _split_bf16(m, 3), 0, 0.0)
        bc = _dot(lhs_b, bsel_ref[...])

        for p in range(N_PAIRS):
            lanes = slice(p * LANES, (p + 1) * LANES)
            s2 = bc[:, 2 * p * LANES:2 * (p + 1) * LANES]
            w2 = bc[:, (2 * N_PAIRS + p) * LANES:(2 * N_PAIRS + p + 1) * LANES]
            m2 = bc[:, (3 * N_PAIRS + p) * LANES:(3 * N_PAIRS + p + 1) * LANES]
            q2 = qkb_ref[rows, p * LANES:(p + 1) * LANES]
            k2 = qkb_ref[rows, D_GROUP + p * LANES:D_GROUP + (p + 1) * LANES]
            vaug = jnp.concatenate([vm_ref[rows, lanes], ones_bf], axis=1)
            zk = jnp.zeros_like(k2)
            zv = jnp.zeros_like(vaug)
            k_cat = jnp.concatenate([jnp.where(first128, k2, zk), jnp.where(first128, zk, k2)], axis=0)
            v_cat = jnp.concatenate([jnp.where(first256, vaug, zv), jnp.where(first256, zv, vaug)], axis=0)
            e2 = e_all[:, 2 * p * L:2 * (p + 1) * L]
            w = (jnp.exp(jnp.where(tril2, e2, _NEG)) * _dot_nt(q2, k_cat)).astype(_BF16)
            c_old = c_ref[p]
            tot = s2 * _dot(q2, c_old.astype(_BF16)) + _dot(w, v_cat)
            den = jnp.maximum(jnp.abs(tot[:, LANES:]), jnp.exp(-m2))
            y = _pair_rmsnorm(tot[:, :LANES] / den, og_ref[:, lanes])
            o = om_ref[rows, lanes]
            y_ref[rows, lanes] = (y * (1.0 / (1.0 + jnp.exp(-o)))).astype(_BF16)
            kw = (k2.astype(_F32) * w2).astype(_BF16)
            c_ref[p] = s2[L - 1:L, :] * c_old + jnp.where(c_mask, _dot_tn(kw, vaug), 0.0)


def _mlstm_selectors():
    L = MLSTM_CHUNK
    row = jnp.arange(LANES)[:, None]
    head, rep = row % N_HEADS, row // N_HEADS
    col_head = jnp.arange(N_HEADS * L)[None, :] // L
    esel = jnp.where(head == col_head, jnp.where(rep < 3, 1, jnp.where(rep < 6, 2, 0)), 0).astype(_BF16)
    col = jnp.arange(4 * N_PAIRS * LANES)[None, :]
    grp = col // LANES
    pair = jnp.where(grp < 2 * N_PAIRS, grp // 2, (grp - 2 * N_PAIRS) % N_PAIRS)
    col_head = 2 * pair + (col % LANES) // HEAD_DIM
    src = jnp.where(grp < 2 * N_PAIRS, 0, jnp.where(grp < 3 * N_PAIRS, 1, 2))
    row_src = jnp.where(rep < 3, 0, jnp.where(rep < 5, 1, 2))
    bsel = ((head == col_head) & (row_src == src)).astype(_BF16)
    return esel, bsel


def _mlstm(qk, vm, om, gates, conv_w, conv_b, gate_b, out_g, esel, bsel):
    bsz, s, _ = qk.shape
    ts = min(MLSTM_TILE, s)
    tok = lambda width: pl.BlockSpec((None, ts, width), lambda b, i: (b, i, 0))
    full = lambda shape: pl.BlockSpec(shape, lambda b, i: (0, 0))
    return pl.pallas_call(
        _mlstm_kernel,
        grid=(bsz, s // ts),
        in_specs=[tok(2 * D_GROUP), tok(D_GROUP), tok(D_GROUP), tok(2 * LANES),
                  full((CONV_WIDTH, 2 * D_GROUP)), full((1, 2 * D_GROUP)), full((1, 2 * LANES)),
                  full((1, D_GROUP)), full(esel.shape), full(bsel.shape)],
        out_specs=tok(D_GROUP),
        out_shape=jax.ShapeDtypeStruct((bsz, s, D_GROUP), _BF16),
        scratch_shapes=[pltpu.VMEM((N_PAIRS, LANES, 2 * LANES), _F32),
                        pltpu.VMEM((1, LANES), _F32),
                        pltpu.VMEM((ts + 8, 2 * D_GROUP), _F32),
                        pltpu.VMEM((ts, 2 * D_GROUP), _BF16)],
        compiler_params=pltpu.CompilerParams(dimension_semantics=("parallel", "arbitrary"),
                                             vmem_limit_bytes=VMEM_LIMIT),
        name="mlstm",
    )(qk, vm, om, gates, conv_w, conv_b, gate_b, out_g, esel, bsel)


def _sb_kernel(q_ref, k_ref, v_ref, t_ref, og_ref, y_ref, acc_ref, tot_ref):
    tq = min(SB_TILE, q_ref.shape[0])
    group = q_ref.shape[0] // tq
    lane = lax.broadcasted_iota(jnp.int32, (tq, LANES), 1)
    first = lane < HEAD_DIM
    r_i = lax.broadcasted_iota(jnp.int32, (tq, tq), 0)
    c_i = lax.broadcasted_iota(jnp.int32, (tq, tq), 1)
    causal = c_i < r_i

    def block_terms(q_e, j, mask):
        start = pl.multiple_of(j * tq, tq)
        kb = k_ref[pl.ds(start, tq), :]
        vb = v_ref[pl.ds(start, tq), :]
        z = _dot_nt(q_e, kb)
        sp = jnp.maximum(z, 0.0) + jnp.log(1.0 + jnp.exp2(-jnp.abs(z))) * LOG2E
        if mask is not None:
            sp = jnp.where(mask, sp, 0.0)
        cum = _dot(jnp.concatenate(_split_bf16(sp, SB_SPLIT), axis=1), t_ref[...])
        d = z - cum
        if mask is not None:
            d = jnp.where(mask, d, _NEG)
        pv = _dot(jnp.exp2(d).astype(_BF16), vb)
        return pv, jnp.broadcast_to(cum[:, 0:1], (tq, LANES))

    q_heads = []
    for g in range(group):
        qi = pl.program_id(2) * group + g
        q2 = q_ref[g * tq:(g + 1) * tq, :]
        zero = jnp.zeros_like(q2)
        q_heads.append((jnp.where(first, q2, zero), jnp.where(first, zero, q2)))
        has_prev = (qi > 0).astype(_F32)
        j_prev = jnp.maximum(qi - 1, 0)
        for e in range(2):
            pv0, rs0 = block_terms(q_heads[g][e], qi, causal)
            pv1, rs1 = block_terms(q_heads[g][e], j_prev, None)
            acc_ref[g, e] = pv0 + (has_prev * jnp.exp2(-rs0)) * pv1
            tot_ref[g, e] = rs0 + has_prev * rs1

    def more(c):
        return jnp.logical_and(c[0] >= 0, c[1] > 0)

    for g in range(group):
        qi = pl.program_id(2) * group + g

        def alive(g=g):
            return (jnp.min(jnp.minimum(tot_ref[g, 0], tot_ref[g, 1])) < SB_DEAD_LOG2).astype(jnp.int32)

        def sweep(c, g=g, alive=alive):
            j = c[0]
            for e in range(2):
                pv, rs = block_terms(q_heads[g][e], j, None)
                tot = tot_ref[g, e]
                acc_ref[g, e] = acc_ref[g, e] + jnp.exp2(-tot) * pv
                tot_ref[g, e] = tot + rs
            return j - 1, alive()

        lax.while_loop(more, sweep, (qi - 2, alive()))
        y = jnp.where(first, acc_ref[g, 0], acc_ref[g, 1])
        y_ref[g * tq:(g + 1) * tq, :] = _pair_rmsnorm(y, og_ref[...]).astype(_BF16)


def _sb_attn(qs, ks, vs, out_g):
    bsz, s, _ = qs.shape
    tq = min(SB_TILE, s)
    group = SB_GROUP if s % (SB_GROUP * tq) == 0 else 1
    rows = group * tq
    tri = (jnp.arange(tq)[:, None] >= jnp.arange(tq)[None, :]).astype(_BF16)
    t_mat = jnp.concatenate([tri] * SB_SPLIT, axis=0)
    return pl.pallas_call(
        _sb_kernel,
        grid=(bsz, N_PAIRS, s // rows),
        in_specs=[pl.BlockSpec((None, rows, LANES), lambda b, p, i: (b, i, p)),
                  pl.BlockSpec((None, s, LANES), lambda b, p, i: (b, 0, p)),
                  pl.BlockSpec((None, s, LANES), lambda b, p, i: (b, 0, p)),
                  pl.BlockSpec((SB_SPLIT * tq, tq), lambda b, p, i: (0, 0)),
                  pl.BlockSpec((1, LANES), lambda b, p, i: (0, p))],
        out_specs=pl.BlockSpec((None, rows, LANES), lambda b, p, i: (b, i, p)),
        out_shape=jax.ShapeDtypeStruct((bsz, s, D_GROUP), _BF16),
        scratch_shapes=[pltpu.VMEM((group, 2, tq, LANES), _F32),
                        pltpu.VMEM((group, 2, tq, LANES), _F32)],
        compiler_params=pltpu.CompilerParams(dimension_semantics=("parallel", "parallel", "arbitrary"),
                                             vmem_limit_bytes=VMEM_LIMIT),
        name="sb_attn",
    )(qs, ks, vs, t_mat, out_g)


def _out_mlp_kernel(x_ref, ym_ref, ys_ref, wo_ref, g_ref, wu_ref, wd_ref, o_ref, a_ref):
    x = x_ref[...] + _dot(ym_ref[...], wo_ref[0:D_GROUP, :]) + _dot(ys_ref[...], wo_ref[D_GROUP:, :])
    ms = jnp.mean(x * x, axis=-1, keepdims=True)
    h = (x * lax.rsqrt(ms + NORM_EPS) * g_ref[...]).astype(_BF16)
    for c in range(D_FF // FF_TILE):
        cols = slice(c * FF_TILE, (c + 1) * FF_TILE)
        up = jnp.maximum(_dot(h, wu_ref[:, cols]), 0.0)
        a_ref[:, cols] = (up * up).astype(_BF16)
    o_ref[...] = x + _dot(a_ref[...], wd_ref[...])


def _out_mlp(x2d, ym, ys, wo, g, wu, wd):
    n = x2d.shape[0]
    tn = TOK_TILE
    row = lambda width: pl.BlockSpec((tn, width), lambda i: (i, 0))
    full = lambda shape: pl.BlockSpec(shape, lambda i: (0, 0))
    return pl.pallas_call(
        _out_mlp_kernel,
        grid=(n // tn,),
        in_specs=[row(D_MODEL), row(D_GROUP), row(D_GROUP), full((D_MODEL, D_MODEL)),
                  full((1, D_MODEL)), full((D_MODEL, D_FF)), full((D_FF, D_MODEL))],
        out_specs=row(D_MODEL),
        out_shape=jax.ShapeDtypeStruct((n, D_MODEL), _F32),
        scratch_shapes=[pltpu.VMEM((tn, D_FF), _BF16)],
        compiler_params=pltpu.CompilerParams(dimension_semantics=("parallel",),
                                             vmem_limit_bytes=VMEM_LIMIT),
        name="out_mlp",
    )(x2d, ym, ys, wo, g, wu, wd)


def _pad_cols(a, width):
    return jnp.pad(a, ((0, 0), (0, width - a.shape[1])))


def kernel(x, attn_norm_g, w_in, conv_w, conv_b, b_igate, b_fgate, q_norm_g, k_norm_g, out_norm_g, w_out,
           mlp_norm_g, w_up, w_down):
    bsz, s, _ = x.shape
    n = bsz * s
    depth = w_in.shape[0]
    o_gate = 4 * D_GROUP
    o_sb = o_gate + 2 * N_HEADS
    x2d = x.reshape(n, D_MODEL)
    esel, bsel = _mlstm_selectors()
    for l in range(depth):
        w = jnp.concatenate([w_in[l][:, :o_gate], w_in[l][:, o_sb:],
                             _pad_cols(jnp.tile(w_in[l][:, o_gate:o_gate + N_HEADS], (1, GATE_REP)), LANES),
                             _pad_cols(jnp.tile(w_in[l][:, o_gate + N_HEADS:o_sb], (1, GATE_REP)), LANES)],
                            axis=1).astype(_BF16)
        qg = jnp.tile(q_norm_g[l] * (HEAD_DIM ** -0.5 * LOG2E), 2)[None, :]
        kg = jnp.tile(k_norm_g[l], 2)[None, :]
        gate_b = jnp.concatenate([_pad_cols(jnp.tile(b_igate[l], GATE_REP)[None, :], LANES),
                                  _pad_cols(jnp.tile(b_fgate[l], GATE_REP)[None, :], LANES)], axis=1)
        qk, vm, om, qs, ks, vs, gates = _in_proj(x2d, attn_norm_g[l][None, :], w, qg, kg)
        tok = lambda a: a.reshape(bsz, s, a.shape[-1])
        ym = _mlstm(tok(qk), tok(vm), tok(om), tok(gates), conv_w[l], conv_b[l][None, :], gate_b,
                    out_norm_g[l][None, :D_GROUP], esel, bsel)
        ys = _sb_attn(tok(qs), tok(ks), tok(vs), out_norm_g[l][None, D_GROUP:])
        x2d = _out_mlp(x2d, ym.reshape(n, D_GROUP), ys.reshape(n, D_GROUP), w_out[l].astype(_BF16),
                       mlp_norm_g[l][None, :], w_up[l].astype(_BF16), w_down[l].astype(_BF16))
    return x2d.reshape(bsz, s, D_MODEL)
```

```python
import jax
import jax.numpy as jnp
from jax import lax
from jax.experimental import pallas as pl
from jax.experimental.pallas import tpu as pltpu

D_MODEL = 1024
HEAD_DIM = 64
N_HEADS = 8
D_GROUP = N_HEADS * HEAD_DIM
N_PAIRS = N_HEADS // 2
D_FF = 4 * D_MODEL
CONV_WIDTH = 4
NORM_EPS = 1e-6
LANES = 128
D_PROJ = 7 * D_GROUP + 2 * LANES

TOK_TILE = 512
FF_TILE = 512
MLSTM_TILE = 512
MLSTM_CHUNK = 128
GATE_REP = 8
SB_TILE = 256
SB_GROUP = 4
SB_SPLIT = 2
SB_DEAD_LOG2 = 160.0
LOG2E = 1.4426950408889634
VMEM_LIMIT = 56 * 1024 * 1024

_F32 = jnp.float32
_BF16 = jnp.bfloat16
_NEG = -1e30


def _dot(a, b):
    return jnp.dot(a, b, preferred_element_type=_F32)


def _dot_nt(a, b):
    return lax.dot_general(a, b, (((1,), (1,)), ((), ())), preferred_element_type=_F32)


def _dot_tn(a, b):
    return lax.dot_general(a, b, (((0,), (0,)), ((), ())), preferred_element_type=_F32)


def _split_bf16(a, pieces):
    parts = []
    r = a
    for i in range(pieces):
        p = r.astype(_BF16)
        parts.append(p)
        if i + 1 < pieces:
            r = r - p.astype(_F32)
    return parts


def _pair_rmsnorm(x, gain):
    lane = lax.broadcasted_iota(jnp.int32, x.shape, 1)
    first = lane < HEAD_DIM
    sq = x * x
    s_a = jnp.sum(jnp.where(first, sq, 0.0), axis=-1, keepdims=True)
    s_b = jnp.sum(jnp.where(first, 0.0, sq), axis=-1, keepdims=True)
    ms = jnp.where(first, s_a, s_b) * (1.0 / HEAD_DIM)
    return x * lax.rsqrt(ms + NORM_EPS) * gain


def _mix_in_kernel(x_ref, g_ref, w_ref, qg_ref, kg_ref, cw_ref, cb_ref, gb_ref, og_ref, esel_ref, bsel_ref,
                   y_ref, qs_ref, ks_ref, vs_ref,
                   c_ref, m_ref, ubuf_ref, qkb_ref, vm_ref, om_ref, gate_ref):
    ts = x_ref.shape[0]
    L = MLSTM_CHUNK

    @pl.when(pl.program_id(1) == 0)
    def _():
        c_ref[...] = jnp.zeros_like(c_ref)
        m_ref[...] = jnp.zeros_like(m_ref)
        ubuf_ref[0:8, :] = jnp.zeros((8, 2 * D_GROUP), _F32)

    x = x_ref[...]
    ms = jnp.mean(x * x, axis=-1, keepdims=True)
    h = (x * lax.rsqrt(ms + NORM_EPS) * g_ref[...]).astype(_BF16)

    def proj(c):
        return _dot(h, w_ref[:, c * D_GROUP:(c + 1) * D_GROUP])

    ubuf_ref[8:, 0:D_GROUP] = proj(0)
    ubuf_ref[8:, D_GROUP:2 * D_GROUP] = proj(1)
    vm_ref[...] = proj(2).astype(_BF16)
    gate_ref[...] = _dot(h, w_ref[:, 7 * D_GROUP:D_PROJ])
    om_ref[...] = proj(3)

    acc = cb_ref[...] + ubuf_ref[8:, :] * cw_ref[CONV_WIDTH - 1:CONV_WIDTH, :]
    for j in range(1, CONV_WIDTH):
        acc = acc + ubuf_ref[8 - j:8 - j + ts, :] * cw_ref[CONV_WIDTH - 1 - j:CONV_WIDTH - j, :]
    ubuf_ref[0:8, :] = ubuf_ref[ts:ts + 8, :]
    act = acc * (1.0 / (1.0 + jnp.exp(-acc)))
    qkb_ref[:, :D_GROUP] = (act[:, :D_GROUP] * (HEAD_DIM ** -0.5)).astype(_BF16)
    qkb_ref[:, D_GROUP:] = act[:, D_GROUP:].astype(_BF16)

    r_i = lax.broadcasted_iota(jnp.int32, (L, L), 0)
    c_i = lax.broadcasted_iota(jnp.int32, (L, L), 1)
    tril_bf = jnp.where(c_i <= r_i, 1.0, 0.0).astype(_BF16)
    tril3 = jnp.concatenate([tril_bf] * 3, axis=1)
    r2 = lax.broadcasted_iota(jnp.int32, (L, 2 * L), 0)
    c2 = lax.broadcasted_iota(jnp.int32, (L, 2 * L), 1)
    tril2 = (c2 % L) <= r2
    row128 = lax.broadcasted_iota(jnp.int32, (L, LANES), 0)
    lane128 = lax.broadcasted_iota(jnp.int32, (L, LANES), 1)
    lane256 = lax.broadcasted_iota(jnp.int32, (L, 2 * LANES), 1)
    first128 = lane128 < HEAD_DIM
    first256 = (lane256 // HEAD_DIM) % 2 == 0
    crow = lax.broadcasted_iota(jnp.int32, (LANES, 2 * LANES), 0)
    ccol = lax.broadcasted_iota(jnp.int32, (LANES, 2 * LANES), 1)
    c_mask = (crow // HEAD_DIM) == ((ccol // HEAD_DIM) % 2)
    ones_bf = jnp.ones((L, LANES), _BF16)
    rep = lane128 // N_HEADS

    def by_copy(pieces, first_copy, fill):
        out = jnp.full((L, LANES), fill, pieces[0].dtype)
        for k, piece in enumerate(pieces):
            out = jnp.where(rep == first_copy + k, piece, out)
        return out

    for c in range(ts // L):
        rows = slice(c * L, (c + 1) * L)
        gates = gate_ref[rows, :] + gb_ref[...]
        ic = gates[:, :LANES]
        fp = gates[:, LANES:]
        lf = jnp.minimum(fp, 0.0) - jnp.log(1.0 + jnp.exp(-jnp.abs(fp)))
        b = _dot(tril3, jnp.concatenate(_split_bf16(lf, 3), axis=0))
        r = ic - b
        run = r
        k = 1
        while k < L:
            run = jnp.maximum(run, jnp.where(row128 >= k, pltpu.roll(run, k, axis=0), _NEG))
            k *= 2
        m_prev = m_ref[...]
        big_m = jnp.maximum(run, m_prev)
        m = b + big_m
        s_inter = jnp.exp(m_prev - big_m)
        w_s = jnp.exp(r - big_m[L - 1:L, :])
        m_ref[...] = m[L - 1:L, :]

        lhs_e = by_copy(_split_bf16(-big_m, 3) + [ones_bf] * 3, 0, 0.0)
        r_pieces = by_copy([p.astype(_F32) for p in _split_bf16(r, 3)], 3, 0.0)
        r_t = jnp.concatenate([r_pieces.T.astype(_BF16)] * N_HEADS, axis=1)
        esel = esel_ref[...]
        rhs_e = jnp.where(esel == 1, jnp.ones_like(r_t), jnp.where(esel == 2, r_t, jnp.zeros_like(r_t)))
        e_all = _dot(lhs_e, rhs_e)
        lhs_b = by_copy(_split_bf16(s_inter, 3) + _split_bf16(w_s, 2) + _split_bf16(m, 3), 0, 0.0)
        bc = _dot(lhs_b, bsel_ref[...])

        for p in range(N_PAIRS):
            lanes = slice(p * LANES, (p + 1) * LANES)
            s2 = bc[:, 2 * p * LANES:2 * (p + 1) * LANES]
            w2 = bc[:, (2 * N_PAIRS + p) * LANES:(2 * N_PAIRS + p + 1) * LANES]
            m2 = bc[:, (3 * N_PAIRS + p) * LANES:(3 * N_PAIRS + p + 1) * LANES]
            q2 = qkb_ref[rows, p * LANES:(p + 1) * LANES]
            k2 = qkb_ref[rows, D_GROUP + p * LANES:D_GROUP + (p + 1) * LANES]
            vaug = jnp.concatenate([vm_ref[rows, lanes], ones_bf], axis=1)
            zk = jnp.zeros_like(k2)
            zv = jnp.zeros_like(vaug)
            k_cat = jnp.concatenate([jnp.where(first128, k2, zk), jnp.where(first128, zk, k2)], axis=0)
            v_cat = jnp.concatenate([jnp.where(first256, vaug, zv), jnp.where(first256, zv, vaug)], axis=0)
            e2 = e_all[:, 2 * p * L:2 * (p + 1) * L]
            w = (jnp.exp(jnp.where(tril2, e2, _NEG)) * _dot_nt(q2, k_cat)).astype(_BF16)
            c_old = c_ref[p]
            tot = s2 * _dot(q2, c_old.astype(_BF16)) + _dot(w, v_cat)
            den = jnp.maximum(jnp.abs(tot[:, LANES:]), jnp.exp(-m2))
            y = _pair_rmsnorm(tot[:, :LANES] / den, og_ref[:, lanes])
            o = om_ref[rows, lanes]
            y_ref[rows, lanes] = (y * (1.0 / (1.0 + jnp.exp(-o)))).astype(_BF16)
            kw = (k2.astype(_F32) * w2).astype(_BF16)
            c_ref[p] = s2[L - 1:L, :] * c_old + jnp.where(c_mask, _dot_tn(kw, vaug), 0.0)

    for c, gain_ref, out_ref in ((4, qg_ref, qs_ref), (5, kg_ref, ks_ref)):
        r = proj(c)
        for p in range(N_PAIRS):
            sl = slice(p * LANES, (p + 1) * LANES)
            out_ref[:, sl] = _pair_rmsnorm(r[:, sl], gain_ref[...]).astype(_BF16)
    vs_ref[...] = proj(6).astype(_BF16)


def _mlstm_selectors():
    L = MLSTM_CHUNK
    row = jnp.arange(LANES)[:, None]
    head, rep = row % N_HEADS, row // N_HEADS
    col_head = jnp.arange(N_HEADS * L)[None, :] // L
    esel = jnp.where(head == col_head, jnp.where(rep < 3, 1, jnp.where(rep < 6, 2, 0)), 0).astype(_BF16)
    col = jnp.arange(4 * N_PAIRS * LANES)[None, :]
    grp = col // LANES
    pair = jnp.where(grp < 2 * N_PAIRS, grp // 2, (grp - 2 * N_PAIRS) % N_PAIRS)
    col_head = 2 * pair + (col % LANES) // HEAD_DIM
    src = jnp.where(grp < 2 * N_PAIRS, 0, jnp.where(grp < 3 * N_PAIRS, 1, 2))
    row_src = jnp.where(rep < 3, 0, jnp.where(rep < 5, 1, 2))
    bsel = ((head == col_head) & (row_src == src)).astype(_BF16)
    return esel, bsel


def _mix_in(x, g, w, qg, kg, conv_w, conv_b, gate_b, out_g, esel, bsel):
    bsz, s, _ = x.shape
    ts = min(MLSTM_TILE, s)
    tok = lambda width: pl.BlockSpec((None, ts, width), lambda b, i: (b, i, 0))
    full = lambda shape: pl.BlockSpec(shape, lambda b, i: (0, 0))
    act = jax.ShapeDtypeStruct((bsz, s, D_GROUP), _BF16)
    return pl.pallas_call(
        _mix_in_kernel,
        grid=(bsz, s // ts),
        in_specs=[tok(D_MODEL), full((1, D_MODEL)), full((D_MODEL, D_PROJ)), full((1, LANES)), full((1, LANES)),
                  full((CONV_WIDTH, 2 * D_GROUP)), full((1, 2 * D_GROUP)), full((1, 2 * LANES)),
                  full((1, D_GROUP)), full(esel.shape), full(bsel.shape)],
        out_specs=[tok(D_GROUP)] * 4,
        out_shape=[act] * 4,
        scratch_shapes=[pltpu.VMEM((N_PAIRS, LANES, 2 * LANES), _F32),
                        pltpu.VMEM((1, LANES), _F32),
                        pltpu.VMEM((ts + 8, 2 * D_GROUP), _F32),
                        pltpu.VMEM((ts, 2 * D_GROUP), _BF16),
                        pltpu.VMEM((ts, D_GROUP), _BF16),
                        pltpu.VMEM((ts, D_GROUP), _F32),
                        pltpu.VMEM((ts, 2 * LANES), _F32)],
        compiler_params=pltpu.CompilerParams(dimension_semantics=("parallel", "arbitrary"),
                                             vmem_limit_bytes=VMEM_LIMIT),
        name="mix_in",
    )(x, g, w, qg, kg, conv_w, conv_b, gate_b, out_g, esel, bsel)


def _sb_kernel(q_ref, k_ref, v_ref, t_ref, og_ref, y_ref, acc_ref, tot_ref):
    tq = min(SB_TILE, q_ref.shape[0])
    group = q_ref.shape[0] // tq
    lane = lax.broadcasted_iota(jnp.int32, (tq, LANES), 1)
    first = lane < HEAD_DIM
    r_i = lax.broadcasted_iota(jnp.int32, (tq, tq), 0)
    c_i = lax.broadcasted_iota(jnp.int32, (tq, tq), 1)
    causal = c_i < r_i

    def block_terms(q_e, j, mask):
        start = pl.multiple_of(j * tq, tq)
        kb = k_ref[pl.ds(start, tq), :]
        vb = v_ref[pl.ds(start, tq), :]
        z = _dot_nt(q_e, kb)
        sp = jnp.maximum(z, 0.0) + jnp.log(1.0 + jnp.exp2(-jnp.abs(z))) * LOG2E
        if mask is not None:
            sp = jnp.where(mask, sp, 0.0)
        cum = _dot(jnp.concatenate(_split_bf16(sp, SB_SPLIT), axis=1), t_ref[...])
        d = z - cum
        if mask is not None:
            d = jnp.where(mask, d, _NEG)
        pv = _dot(jnp.exp2(d).astype(_BF16), vb)
        return pv, jnp.broadcast_to(cum[:, 0:1], (tq, LANES))

    q_heads = []
    for g in range(group):
        qi = pl.program_id(2) * group + g
        q2 = q_ref[g * tq:(g + 1) * tq, :]
        zero = jnp.zeros_like(q2)
        q_heads.append((jnp.where(first, q2, zero), jnp.where(first, zero, q2)))
        has_prev = (qi > 0).astype(_F32)
        j_prev = jnp.maximum(qi - 1, 0)
        for e in range(2):
            pv0, rs0 = block_terms(q_heads[g][e], qi, causal)
            pv1, rs1 = block_terms(q_heads[g][e], j_prev, None)
            acc_ref[g, e] = pv0 + (has_prev * jnp.exp2(-rs0)) * pv1
            tot_ref[g, e] = rs0 + has_prev * rs1

    def more(c):
        return jnp.logical_and(c[0] >= 0, c[1] > 0)

    for g in range(group):
        qi = pl.program_id(2) * group + g

        def alive(g=g):
            return (jnp.min(jnp.minimum(tot_ref[g, 0], tot_ref[g, 1])) < SB_DEAD_LOG2).astype(jnp.int32)

        def sweep(c, g=g, alive=alive):
            j = c[0]
            for e in range(2):
                pv, rs = block_terms(q_heads[g][e], j, None)
                tot = tot_ref[g, e]
                acc_ref[g, e] = acc_ref[g, e] + jnp.exp2(-tot) * pv
                tot_ref[g, e] = tot + rs
            return j - 1, alive()

        lax.while_loop(more, sweep, (qi - 2, alive()))
        y = jnp.where(first, acc_ref[g, 0], acc_ref[g, 1])
        y_ref[g * tq:(g + 1) * tq, :] = _pair_rmsnorm(y, og_ref[...]).astype(_BF16)


def _sb_attn(qs, ks, vs, out_g):
    bsz, s, _ = qs.shape
    tq = min(SB_TILE, s)
    group = SB_GROUP if s % (SB_GROUP * tq) == 0 else 1
    rows = group * tq
    tri = (jnp.arange(tq)[:, None] >= jnp.arange(tq)[None, :]).astype(_BF16)
    t_mat = jnp.concatenate([tri] * SB_SPLIT, axis=0)
    return pl.pallas_call(
        _sb_kernel,
        grid=(bsz, N_PAIRS, s // rows),
        in_specs=[pl.BlockSpec((None, rows, LANES), lambda b, p, i: (b, i, p)),
                  pl.BlockSpec((None, s, LANES), lambda b, p, i: (b, 0, p)),
                  pl.BlockSpec((None, s, LANES), lambda b, p, i: (b, 0, p)),
                  pl.BlockSpec((SB_SPLIT * tq, tq), lambda b, p, i: (0, 0)),
                  pl.BlockSpec((1, LANES), lambda b, p, i: (0, p))],
        out_specs=pl.BlockSpec((None, rows, LANES), lambda b, p, i: (b, i, p)),
        out_shape=jax.ShapeDtypeStruct((bsz, s, D_GROUP), _BF16),
        scratch_shapes=[pltpu.VMEM((group, 2, tq, LANES), _F32),
                        pltpu.VMEM((group, 2, tq, LANES), _F32)],
        compiler_params=pltpu.CompilerParams(dimension_semantics=("parallel", "parallel", "arbitrary"),
                                             vmem_limit_bytes=VMEM_LIMIT),
        name="sb_attn",
    )(qs, ks, vs, t_mat, out_g)


def _out_mlp_kernel(x_ref, ym_ref, ys_ref, wo_ref, g_ref, wu_ref, wd_ref, o_ref, a_ref):
    x = x_ref[...] + _dot(ym_ref[...], wo_ref[0:D_GROUP, :]) + _dot(ys_ref[...], wo_ref[D_GROUP:, :])
    ms = jnp.mean(x * x, axis=-1, keepdims=True)
    h = (x * lax.rsqrt(ms + NORM_EPS) * g_ref[...]).astype(_BF16)
    for c in range(D_FF // FF_TILE):
        cols = slice(c * FF_TILE, (c + 1) * FF_TILE)
        up = jnp.maximum(_dot(h, wu_ref[:, cols]), 0.0)
        a_ref[:, cols] = (up * up).astype(_BF16)
    o_ref[...] = x + _dot(a_ref[...], wd_ref[...])


def _out_mlp(x2d, ym, ys, wo, g, wu, wd):
    n = x2d.shape[0]
    tn = TOK_TILE
    row = lambda width: pl.BlockSpec((tn, width), lambda i: (i, 0))
    full = lambda shape: pl.BlockSpec(shape, lambda i: (0, 0))
    return pl.pallas_call(
        _out_mlp_kernel,
        grid=(n // tn,),
        in_specs=[row(D_MODEL), row(D_GROUP), row(D_GROUP), full((D_MODEL, D_MODEL)),
                  full((1, D_MODEL)), full((D_MODEL, D_FF)), full((D_FF, D_MODEL))],
        out_specs=row(D_MODEL),
        out_shape=jax.ShapeDtypeStruct((n, D_MODEL), _F32),
        scratch_shapes=[pltpu.VMEM((tn, D_FF), _BF16)],
        compiler_params=pltpu.CompilerParams(dimension_semantics=("parallel",),
                                             vmem_limit_bytes=VMEM_LIMIT),
        name="out_mlp",
    )(x2d, ym, ys, wo, g, wu, wd)


def _pad_cols(a, width):
    return jnp.pad(a, ((0, 0), (0, width - a.shape[1])))


def kernel(x, attn_norm_g, w_in, conv_w, conv_b, b_igate, b_fgate, q_norm_g, k_norm_g, out_norm_g, w_out,
           mlp_norm_g, w_up, w_down):
    bsz, s, _ = x.shape
    n = bsz * s
    depth = w_in.shape[0]
    o_gate = 4 * D_GROUP
    o_sb = o_gate + 2 * N_HEADS
    x2d = x.reshape(n, D_MODEL)
    esel, bsel = _mlstm_selectors()
    for l in range(depth):
        w = jnp.concatenate([w_in[l][:, :o_gate], w_in[l][:, o_sb:],
                             _pad_cols(jnp.tile(w_in[l][:, o_gate:o_gate + N_HEADS], (1, GATE_REP)), LANES),
                             _pad_cols(jnp.tile(w_in[l][:, o_gate + N_HEADS:o_sb], (1, GATE_REP)), LANES)],
                            axis=1).astype(_BF16)
        qg = jnp.tile(q_norm_g[l] * (HEAD_DIM ** -0.5 * LOG2E), 2)[None, :]
        kg = jnp.tile(k_norm_g[l], 2)[None, :]
        gate_b = jnp.concatenate([_pad_cols(jnp.tile(b_igate[l], GATE_REP)[None, :], LANES),
                                  _pad_cols(jnp.tile(b_fgate[l], GATE_REP)[None, :], LANES)], axis=1)
        ym, qs, ks, vs = _mix_in(x2d.reshape(bsz, s, D_MODEL), attn_norm_g[l][None, :], w, qg, kg, conv_w[l],
                                 conv_b[l][None, :], gate_b, out_norm_g[l][None, :D_GROUP], esel, bsel)
        ys = _sb_attn(qs, ks, vs, out_norm_g[l][None, D_GROUP:])
        x2d = _out_mlp(x2d, ym.reshape(n, D_GROUP), ys.reshape(n, D_GROUP), w_out[l].astype(_BF16),
                       mlp_norm_g[l][None, :], w_up[l].astype(_BF16), w_down[l].astype(_BF16))
    return x2d.reshape(bsz, s, D_MODEL)
```

```python
import jax
import jax.numpy as jnp
from jax import lax
from jax.experimental import pallas as pl
from jax.experimental.pallas import tpu as pltpu

D_MODEL = 1024
HEAD_DIM = 64
N_HEADS = 8
D_GROUP = N_HEADS * HEAD_DIM
N_PAIRS = N_HEADS // 2
D_FF = 4 * D_MODEL
CONV_WIDTH = 4
NORM_EPS = 1e-6
LANES = 128
D_PROJ = 7 * D_GROUP + 2 * LANES

TOK_TILE = 512
FF_TILE = 512
MLSTM_TILE = 512
MLSTM_CHUNK = 128
GATE_REP = 8
SB_TILE = 256
SB_GROUP = 4
SB_SPLIT = 2
SB_DEAD_LOG2 = 160.0
LOG2E = 1.4426950408889634
VMEM_LIMIT = 56 * 1024 * 1024

_F32 = jnp.float32
_BF16 = jnp.bfloat16
_NEG = -1e30


def _dot(a, b):
    return jnp.dot(a, b, preferred_element_type=_F32)


def _dot_nt(a, b):
    return lax.dot_general(a, b, (((1,), (1,)), ((), ())), preferred_element_type=_F32)


def _dot_tn(a, b):
    return lax.dot_general(a, b, (((0,), (0,)), ((), ())), preferred_element_type=_F32)


def _split_bf16(a, pieces):
    parts = []
    r = a
    for i in range(pieces):
        p = r.astype(_BF16)
        parts.append(p)
        if i + 1 < pieces:
            r = r - p.astype(_F32)
    return parts


def _pair_rmsnorm(x, gain):
    lane = lax.broadcasted_iota(jnp.int32, x.shape, 1)
    first = lane < HEAD_DIM
    sq = x * x
    s_a = jnp.sum(jnp.where(first, sq, 0.0), axis=-1, keepdims=True)
    s_b = jnp.sum(jnp.where(first, 0.0, sq), axis=-1, keepdims=True)
    ms = jnp.where(first, s_a, s_b) * (1.0 / HEAD_DIM)
    return x * lax.rsqrt(ms + NORM_EPS) * gain


def _mix_in_kernel(x_ref, g_ref, w_ref, qg_ref, kg_ref, cw_ref, cb_ref, gb_ref, og_ref, esel_ref, bsel_ref,
                   y_ref, qs_ref, ks_ref, vs_ref,
                   c_ref, m_ref, ubuf_ref, qkb_ref, vm_ref, om_ref, gate_ref):
    ts = x_ref.shape[0]
    L = MLSTM_CHUNK

    @pl.when(pl.program_id(1) == 0)
    def _():
        c_ref[...] = jnp.zeros_like(c_ref)
        m_ref[...] = jnp.zeros_like(m_ref)
        ubuf_ref[0:8, :] = jnp.zeros((8, 2 * D_GROUP), _F32)

    x = x_ref[...]
    ms = jnp.mean(x * x, axis=-1, keepdims=True)
    h = (x * lax.rsqrt(ms + NORM_EPS) * g_ref[...]).astype(_BF16)

    def proj(c):
        return _dot(h, w_ref[:, c * D_GROUP:(c + 1) * D_GROUP])

    ubuf_ref[8:, 0:D_GROUP] = proj(0)
    ubuf_ref[8:, D_GROUP:2 * D_GROUP] = proj(1)
    vm_ref[...] = proj(2).astype(_BF16)
    gate_ref[...] = _dot(h, w_ref[:, 7 * D_GROUP:D_PROJ])
    om_ref[...] = proj(3)

    acc = cb_ref[...] + ubuf_ref[8:, :] * cw_ref[CONV_WIDTH - 1:CONV_WIDTH, :]
    for j in range(1, CONV_WIDTH):
        acc = acc + ubuf_ref[8 - j:8 - j + ts, :] * cw_ref[CONV_WIDTH - 1 - j:CONV_WIDTH - j, :]
    ubuf_ref[0:8, :] = ubuf_ref[ts:ts + 8, :]
    act = acc * (1.0 / (1.0 + jnp.exp(-acc)))
    qkb_ref[:, :D_GROUP] = (act[:, :D_GROUP] * (HEAD_DIM ** -0.5)).astype(_BF16)
    qkb_ref[:, D_GROUP:] = act[:, D_GROUP:].astype(_BF16)

    r_i = lax.broadcasted_iota(jnp.int32, (L, L), 0)
    c_i = lax.broadcasted_iota(jnp.int32, (L, L), 1)
    tril_bf = jnp.where(c_i <= r_i, 1.0, 0.0).astype(_BF16)
    tril3 = jnp.concatenate([tril_bf] * 3, axis=1)
    r2 = lax.broadcasted_iota(jnp.int32, (L, 2 * L), 0)
    c2 = lax.broadcasted_iota(jnp.int32, (L, 2 * L), 1)
    tril2 = (c2 % L) <= r2
    row128 = lax.broadcasted_iota(jnp.int32, (L, LANES), 0)
    lane128 = lax.broadcasted_iota(jnp.int32, (L, LANES), 1)
    lane256 = lax.broadcasted_iota(jnp.int32, (L, 2 * LANES), 1)
    first128 = lane128 < HEAD_DIM
    first256 = (lane256 // HEAD_DIM) % 2 == 0
    crow = lax.broadcasted_iota(jnp.int32, (LANES, 2 * LANES), 0)
    ccol = lax.broadcasted_iota(jnp.int32, (LANES, 2 * LANES), 1)
    c_mask = (crow // HEAD_DIM) == ((ccol // HEAD_DIM) % 2)
    ones_bf = jnp.ones((L, LANES), _BF16)
    rep = lane128 // N_HEADS

    def by_copy(pieces, first_copy, fill):
        out = jnp.full((L, LANES), fill, pieces[0].dtype)
        for k, piece in enumerate(pieces):
            out = jnp.where(rep == first_copy + k, piece, out)
        return out

    def sb_piece(i):
        c, half = 4 + i // 2, i % 2
        r = _dot(h, w_ref[:, c * D_GROUP + half * 2 * LANES:c * D_GROUP + (half + 1) * 2 * LANES])
        if c == 6:
            vs_ref[:, half * 2 * LANES:(half + 1) * 2 * LANES] = r.astype(_BF16)
            return
        gain_ref, out_ref = (qg_ref, qs_ref) if c == 4 else (kg_ref, ks_ref)
        for p in range(2):
            sl = slice((2 * half + p) * LANES, (2 * half + p + 1) * LANES)
            out_ref[:, sl] = _pair_rmsnorm(r[:, p * LANES:(p + 1) * LANES], gain_ref[...]).astype(_BF16)

    pieces_done = 0
    for c in range(ts // L):
        rows = slice(c * L, (c + 1) * L)
        gates = gate_ref[rows, :] + gb_ref[...]
        ic = gates[:, :LANES]
        fp = gates[:, LANES:]
        lf = jnp.minimum(fp, 0.0) - jnp.log(1.0 + jnp.exp(-jnp.abs(fp)))
        b = _dot(tril3, jnp.concatenate(_split_bf16(lf, 3), axis=0))
        r = ic - b
        run = r
        k = 1
        while k < L:
            run = jnp.maximum(run, jnp.where(row128 >= k, pltpu.roll(run, k, axis=0), _NEG))
            k *= 2
        m_prev = m_ref[...]
        big_m = jnp.maximum(run, m_prev)
        m = b + big_m
        s_inter = jnp.exp(m_prev - big_m)
        w_s = jnp.exp(r - big_m[L - 1:L, :])
        m_ref[...] = m[L - 1:L, :]

        lhs_e = by_copy(_split_bf16(-big_m, 3) + [ones_bf] * 3, 0, 0.0)
        r_pieces = by_copy([p.astype(_F32) for p in _split_bf16(r, 3)], 3, 0.0)
        r_t = jnp.concatenate([r_pieces.T.astype(_BF16)] * N_HEADS, axis=1)
        esel = esel_ref[...]
        rhs_e = jnp.where(esel == 1, jnp.ones_like(r_t), jnp.where(esel == 2, r_t, jnp.zeros_like(r_t)))
        e_all = _dot(lhs_e, rhs_e)
        lhs_b = by_copy(_split_bf16(s_inter, 3) + _split_bf16(w_s, 2) + _split_bf16(m, 3), 0, 0.0)
        bc = _dot(lhs_b, bsel_ref[...])

        for p in range(N_PAIRS):
            lanes = slice(p * LANES, (p + 1) * LANES)
            s2 = bc[:, 2 * p * LANES:2 * (p + 1) * LANES]
            w2 = bc[:, (2 * N_PAIRS + p) * LANES:(2 * N_PAIRS + p + 1) * LANES]
            m2 = bc[:, (3 * N_PAIRS + p) * LANES:(3 * N_PAIRS + p + 1) * LANES]
            q2 = qkb_ref[rows, p * LANES:(p + 1) * LANES]
            k2 = qkb_ref[rows, D_GROUP + p * LANES:D_GROUP + (p + 1) * LANES]
            vaug = jnp.concatenate([vm_ref[rows, lanes], ones_bf], axis=1)
            zk = jnp.zeros_like(k2)
            zv = jnp.zeros_like(vaug)
            k_cat = jnp.concatenate([jnp.where(first128, k2, zk), jnp.where(first128, zk, k2)], axis=0)
            v_cat = jnp.concatenate([jnp.where(first256, vaug, zv), jnp.where(first256, zv, vaug)], axis=0)
            e2 = e_all[:, 2 * p * L:2 * (p + 1) * L]
            w = (jnp.exp(jnp.where(tril2, e2, _NEG)) * _dot_nt(q2, k_cat)).astype(_BF16)
            c_old = c_ref[p]
            tot = s2 * _dot(q2, c_old.astype(_BF16)) + _dot(w, v_cat)
            den = jnp.maximum(jnp.abs(tot[:, LANES:]), jnp.exp(-m2))
            y = _pair_rmsnorm(tot[:, :LANES] / den, og_ref[:, lanes])
            o = om_ref[rows, lanes]
            y_ref[rows, lanes] = (y * (1.0 / (1.0 + jnp.exp(-o)))).astype(_BF16)
            kw = (k2.astype(_F32) * w2).astype(_BF16)
            c_ref[p] = s2[L - 1:L, :] * c_old + jnp.where(c_mask, _dot_tn(kw, vaug), 0.0)
            if p % 2 == 1 and pieces_done < 6:
                sb_piece(pieces_done)
                pieces_done += 1
    while pieces_done < 6:
        sb_piece(pieces_done)
        pieces_done += 1


def _mlstm_selectors():
    L = MLSTM_CHUNK
    row = jnp.arange(LANES)[:, None]
    head, rep = row % N_HEADS, row // N_HEADS
    col_head = jnp.arange(N_HEADS * L)[None, :] // L
    esel = jnp.where(head == col_head, jnp.where(rep < 3, 1, jnp.where(rep < 6, 2, 0)), 0).astype(_BF16)
    col = jnp.arange(4 * N_PAIRS * LANES)[None, :]
    grp = col // LANES
    pair = jnp.where(grp < 2 * N_PAIRS, grp // 2, (grp - 2 * N_PAIRS) % N_PAIRS)
    col_head = 2 * pair + (col % LANES) // HEAD_DIM
    src = jnp.where(grp < 2 * N_PAIRS, 0, jnp.where(grp < 3 * N_PAIRS, 1, 2))
    row_src = jnp.where(rep < 3, 0, jnp.where(rep < 5, 1, 2))
    bsel = ((head == col_head) & (row_src == src)).astype(_BF16)
    return esel, bsel


def _mix_in(layer, x, g, w, qg, kg, conv_w, conv_b, gate_b, out_g, esel, bsel):
    bsz, s, _ = x.shape
    ts = min(MLSTM_TILE, s)
    tok = lambda width: pl.BlockSpec((None, ts, width), lambda b, i: (b, i, 0))
    full = lambda shape: pl.BlockSpec(shape, lambda b, i: (0, 0))
    per_layer = lambda a: pl.BlockSpec((None,) + a.shape[1:], lambda b, i: (layer, 0, 0))
    act = jax.ShapeDtypeStruct((bsz, s, D_GROUP), _BF16)
    return pl.pallas_call(
        _mix_in_kernel,
        grid=(bsz, s // ts),
        in_specs=[tok(D_MODEL), per_layer(g), per_layer(w), per_layer(qg), per_layer(kg),
                  per_layer(conv_w), per_layer(conv_b), per_layer(gate_b), per_layer(out_g),
                  full(esel.shape), full(bsel.shape)],
        out_specs=[tok(D_GROUP)] * 4,
        out_shape=[act] * 4,
        scratch_shapes=[pltpu.VMEM((N_PAIRS, LANES, 2 * LANES), _F32),
                        pltpu.VMEM((1, LANES), _F32),
                        pltpu.VMEM((ts + 8, 2 * D_GROUP), _F32),
                        pltpu.VMEM((ts, 2 * D_GROUP), _BF16),
                        pltpu.VMEM((ts, D_GROUP), _BF16),
                        pltpu.VMEM((ts, D_GROUP), _F32),
                        pltpu.VMEM((ts, 2 * LANES), _F32)],
        compiler_params=pltpu.CompilerParams(dimension_semantics=("parallel", "arbitrary"),
                                             vmem_limit_bytes=VMEM_LIMIT),
        name="mix_in",
    )(x, g, w, qg, kg, conv_w, conv_b, gate_b, out_g, esel, bsel)


def _sb_kernel(q_ref, k_ref, v_ref, t_ref, og_ref, y_ref, acc_ref, tot_ref):
    tq = min(SB_TILE, q_ref.shape[0])
    group = q_ref.shape[0] // tq
    lane = lax.broadcasted_iota(jnp.int32, (tq, LANES), 1)
    first = lane < HEAD_DIM
    r_i = lax.broadcasted_iota(jnp.int32, (tq, tq), 0)
    c_i = lax.broadcasted_iota(jnp.int32, (tq, tq), 1)
    causal = c_i < r_i

    def block_terms(q_e, j, mask):
        start = pl.multiple_of(j * tq, tq)
        kb = k_ref[pl.ds(start, tq), :]
        vb = v_ref[pl.ds(start, tq), :]
        z = _dot_nt(q_e, kb)
        sp = jnp.maximum(z, 0.0) + jnp.log(1.0 + jnp.exp2(-jnp.abs(z))) * LOG2E
        if mask is not None:
            sp = jnp.where(mask, sp, 0.0)
        cum = _dot(jnp.concatenate(_split_bf16(sp, SB_SPLIT), axis=1), t_ref[...])
        d = z - cum
        if mask is not None:
            d = jnp.where(mask, d, _NEG)
        pv = _dot(jnp.exp2(d).astype(_BF16), vb)
        return pv, jnp.broadcast_to(cum[:, 0:1], (tq, LANES))

    q_heads = []
    for g in range(group):
        qi = pl.program_id(2) * group + g
        q2 = q_ref[g * tq:(g + 1) * tq, :]
        zero = jnp.zeros_like(q2)
        q_heads.append((jnp.where(first, q2, zero), jnp.where(first, zero, q2)))
        has_prev = (qi > 0).astype(_F32)
        j_prev = jnp.maximum(qi - 1, 0)
        for e in range(2):
            pv0, rs0 = block_terms(q_heads[g][e], qi, causal)
            pv1, rs1 = block_terms(q_heads[g][e], j_prev, None)
            acc_ref[g, e] = pv0 + (has_prev * jnp.exp2(-rs0)) * pv1
            tot_ref[g, e] = rs0 + has_prev * rs1

    def more(c):
        return jnp.logical_and(c[0] >= 0, c[1] > 0)

    for g in range(group):
        qi = pl.program_id(2) * group + g

        def alive(g=g):
            return (jnp.min(jnp.minimum(tot_ref[g, 0], tot_ref[g, 1])) < SB_DEAD_LOG2).astype(jnp.int32)

        def sweep(c, g=g, alive=alive):
            j = c[0]
            for e in range(2):
                pv, rs = block_terms(q_heads[g][e], j, None)
                tot = tot_ref[g, e]
                acc_ref[g, e] = acc_ref[g, e] + jnp.exp2(-tot) * pv
                tot_ref[g, e] = tot + rs
            return j - 1, alive()

        lax.while_loop(more, sweep, (qi - 2, alive()))
        y = jnp.where(first, acc_ref[g, 0], acc_ref[g, 1])
        y_ref[g * tq:(g + 1) * tq, :] = _pair_rmsnorm(y, og_ref[...]).astype(_BF16)


def _sb_attn(layer, qs, ks, vs, out_g):
    bsz, s, _ = qs.shape
    tq = min(SB_TILE, s)
    group = SB_GROUP if s % (SB_GROUP * tq) == 0 else 1
    rows = group * tq
    tri = (jnp.arange(tq)[:, None] >= jnp.arange(tq)[None, :]).astype(_BF16)
    t_mat = jnp.concatenate([tri] * SB_SPLIT, axis=0)
    return pl.pallas_call(
        _sb_kernel,
        grid=(bsz, N_PAIRS, s // rows),
        in_specs=[pl.BlockSpec((None, rows, LANES), lambda b, p, i: (b, i, p)),
                  pl.BlockSpec((None, s, LANES), lambda b, p, i: (b, 0, p)),
                  pl.BlockSpec((None, s, LANES), lambda b, p, i: (b, 0, p)),
                  pl.BlockSpec((SB_SPLIT * tq, tq), lambda b, p, i: (0, 0)),
                  pl.BlockSpec((None, 1, LANES), lambda b, p, i: (layer, 0, N_PAIRS + p))],
        out_specs=pl.BlockSpec((None, rows, LANES), lambda b, p, i: (b, i, p)),
        out_shape=jax.ShapeDtypeStruct((bsz, s, D_GROUP), _BF16),
        scratch_shapes=[pltpu.VMEM((group, 2, tq, LANES), _F32),
                        pltpu.VMEM((group, 2, tq, LANES), _F32)],
        compiler_params=pltpu.CompilerParams(dimension_semantics=("parallel", "parallel", "arbitrary"),
                                             vmem_limit_bytes=VMEM_LIMIT),
        name="sb_attn",
    )(qs, ks, vs, t_mat, out_g)


def _out_mlp_kernel(x_ref, ym_ref, ys_ref, wo_ref, g_ref, wu_ref, wd_ref, o_ref, a_ref):
    x = x_ref[...] + _dot(ym_ref[...], wo_ref[0:D_GROUP, :]) + _dot(ys_ref[...], wo_ref[D_GROUP:, :])
    ms = jnp.mean(x * x, axis=-1, keepdims=True)
    h = (x * lax.rsqrt(ms + NORM_EPS) * g_ref[...]).astype(_BF16)
    for c in range(D_FF // FF_TILE):
        cols = slice(c * FF_TILE, (c + 1) * FF_TILE)
        up = jnp.maximum(_dot(h, wu_ref[:, cols]), 0.0)
        a_ref[:, cols] = (up * up).astype(_BF16)
    o_ref[...] = x + _dot(a_ref[...], wd_ref[...])


def _out_mlp(layer, x2d, ym, ys, wo, g, wu, wd):
    n = x2d.shape[0]
    tn = TOK_TILE
    row = lambda width: pl.BlockSpec((tn, width), lambda i: (i, 0))
    per_layer = lambda a: pl.BlockSpec((None,) + a.shape[1:], lambda i: (layer, 0, 0))
    return pl.pallas_call(
        _out_mlp_kernel,
        grid=(n // tn,),
        in_specs=[row(D_MODEL), row(D_GROUP), row(D_GROUP), per_layer(wo), per_layer(g), per_layer(wu), per_layer(wd)],
        out_specs=row(D_MODEL),
        out_shape=jax.ShapeDtypeStruct((n, D_MODEL), _F32),
        scratch_shapes=[pltpu.VMEM((tn, D_FF), _BF16)],
        compiler_params=pltpu.CompilerParams(dimension_semantics=("parallel",),
                                             vmem_limit_bytes=VMEM_LIMIT),
        name="out_mlp",
    )(x2d, ym, ys, wo, g, wu, wd)


def _pad_last(a, width):
    return jnp.pad(a, [(0, 0)] * (a.ndim - 1) + [(0, width - a.shape[-1])])


def kernel(x, attn_norm_g, w_in, conv_w, conv_b, b_igate, b_fgate, q_norm_g, k_norm_g, out_norm_g, w_out,
           mlp_norm_g, w_up, w_down):
    bsz, s, _ = x.shape
    n = bsz * s
    depth = w_in.shape[0]
    o_gate = 4 * D_GROUP
    o_sb = o_gate + 2 * N_HEADS
    gate_copies = lambda a: _pad_last(jnp.tile(a, (1,) * (a.ndim - 1) + (GATE_REP,)), LANES)
    w = jnp.concatenate([w_in[:, :, :o_gate], w_in[:, :, o_sb:], gate_copies(w_in[:, :, o_gate:o_gate + N_HEADS]),
                         gate_copies(w_in[:, :, o_gate + N_HEADS:o_sb])], axis=2).astype(_BF16)
    vec = lambda a: a[:, None, :]
    qg = vec(jnp.tile(q_norm_g * (HEAD_DIM ** -0.5 * LOG2E), (1, 2)))
    kg = vec(jnp.tile(k_norm_g, (1, 2)))
    gate_b = vec(jnp.concatenate([gate_copies(b_igate), gate_copies(b_fgate)], axis=1))
    wo, wu, wd = w_out.astype(_BF16), w_up.astype(_BF16), w_down.astype(_BF16)
    esel, bsel = _mlstm_selectors()
    x2d = x.reshape(n, D_MODEL)
    for l in range(depth):
        ym, qs, ks, vs = _mix_in(l, x2d.reshape(bsz, s, D_MODEL), vec(attn_norm_g), w, qg, kg, conv_w, vec(conv_b),
                                 gate_b, vec(out_norm_g), esel, bsel)
        ys = _sb_attn(l, qs, ks, vs, vec(out_norm_g))
        x2d = _out_mlp(l, x2d, ym.reshape(n, D_GROUP), ys.reshape(n, D_GROUP), wo, vec(mlp_norm_g), wu, wd)
    return x2d.reshape(bsz, s, D_MODEL)
```

```python
import jax
import jax.numpy as jnp
from jax import lax
from jax.experimental import pallas as pl
from jax.experimental.pallas import tpu as pltpu

D_MODEL = 1024
HEAD_DIM = 64
N_HEADS = 8
D_GROUP = N_HEADS * HEAD_DIM
N_PAIRS = N_HEADS // 2
D_FF = 4 * D_MODEL
CONV_WIDTH = 4
NORM_EPS = 1e-6
LANES = 128
D_PROJ = 7 * D_GROUP + 2 * LANES

TOK_TILE = 512
FF_TILE = 512
MLSTM_TILE = 512
MLSTM_CHUNK = 128
GATE_REP = 8
SB_TILE = 256
SB_GROUP = 4
SB_WAVE = 8
SB_SPLIT = 1
SB_DEAD_LOG2 = 160.0
LOG2E = 1.4426950408889634
VMEM_LIMIT = 56 * 1024 * 1024

_F32 = jnp.float32
_BF16 = jnp.bfloat16
_NEG = -1e30


def _dot(a, b):
    return jnp.dot(a, b, preferred_element_type=_F32)


def _dot_nt(a, b):
    return lax.dot_general(a, b, (((1,), (1,)), ((), ())), preferred_element_type=_F32)


def _dot_tn(a, b):
    return lax.dot_general(a, b, (((0,), (0,)), ((), ())), preferred_element_type=_F32)


def _split_bf16(a, pieces):
    parts = []
    r = a
    for i in range(pieces):
        p = r.astype(_BF16)
        parts.append(p)
        if i + 1 < pieces:
            r = r - p.astype(_F32)
    return parts


def _pair_rmsnorm(x, gain):
    lane = lax.broadcasted_iota(jnp.int32, x.shape, 1)
    first = lane < HEAD_DIM
    sq = x * x
    s_a = jnp.sum(jnp.where(first, sq, 0.0), axis=-1, keepdims=True)
    s_b = jnp.sum(jnp.where(first, 0.0, sq), axis=-1, keepdims=True)
    ms = jnp.where(first, s_a, s_b) * (1.0 / HEAD_DIM)
    return x * lax.rsqrt(ms + NORM_EPS) * gain


def _mix_in_kernel(x_ref, g_ref, w_ref, qg_ref, kg_ref, cw_ref, cb_ref, gb_ref, og_ref, esel_ref, bsel_ref,
                   y_ref, qs_ref, ks_ref, vs_ref,
                   c_ref, m_ref, ubuf_ref, qkb_ref, vm_ref, om_ref, gate_ref):
    ts = x_ref.shape[0]
    L = MLSTM_CHUNK

    @pl.when(pl.program_id(1) == 0)
    def _():
        c_ref[...] = jnp.zeros_like(c_ref)
        m_ref[...] = jnp.zeros_like(m_ref)
        ubuf_ref[0:8, :] = jnp.zeros((8, 2 * D_GROUP), _F32)

    x = x_ref[...]
    ms = jnp.mean(x * x, axis=-1, keepdims=True)
    h = (x * lax.rsqrt(ms + NORM_EPS) * g_ref[...]).astype(_BF16)

    def proj(c):
        return _dot(h, w_ref[:, c * D_GROUP:(c + 1) * D_GROUP])

    ubuf_ref[8:, 0:D_GROUP] = proj(0)
    ubuf_ref[8:, D_GROUP:2 * D_GROUP] = proj(1)
    vm_ref[...] = proj(2).astype(_BF16)
    gate_ref[...] = _dot(h, w_ref[:, 7 * D_GROUP:D_PROJ])
    om_ref[...] = proj(3)

    acc = cb_ref[...] + ubuf_ref[8:, :] * cw_ref[CONV_WIDTH - 1:CONV_WIDTH, :]
    for j in range(1, CONV_WIDTH):
        acc = acc + ubuf_ref[8 - j:8 - j + ts, :] * cw_ref[CONV_WIDTH - 1 - j:CONV_WIDTH - j, :]
    ubuf_ref[0:8, :] = ubuf_ref[ts:ts + 8, :]
    act = acc * (1.0 / (1.0 + jnp.exp(-acc)))
    qkb_ref[:, :D_GROUP] = (act[:, :D_GROUP] * (HEAD_DIM ** -0.5)).astype(_BF16)
    qkb_ref[:, D_GROUP:] = act[:, D_GROUP:].astype(_BF16)

    r_i = lax.broadcasted_iota(jnp.int32, (L, L), 0)
    c_i = lax.broadcasted_iota(jnp.int32, (L, L), 1)
    tril_bf = jnp.where(c_i <= r_i, 1.0, 0.0).astype(_BF16)
    tril3 = jnp.concatenate([tril_bf] * 3, axis=1)
    r2 = lax.broadcasted_iota(jnp.int32, (L, 2 * L), 0)
    c2 = lax.broadcasted_iota(jnp.int32, (L, 2 * L), 1)
    tril2 = (c2 % L) <= r2
    row128 = lax.broadcasted_iota(jnp.int32, (L, LANES), 0)
    lane128 = lax.broadcasted_iota(jnp.int32, (L, LANES), 1)
    lane256 = lax.broadcasted_iota(jnp.int32, (L, 2 * LANES), 1)
    first128 = lane128 < HEAD_DIM
    first256 = (lane256 // HEAD_DIM) % 2 == 0
    crow = lax.broadcasted_iota(jnp.int32, (LANES, 2 * LANES), 0)
    ccol = lax.broadcasted_iota(jnp.int32, (LANES, 2 * LANES), 1)
    c_mask = (crow // HEAD_DIM) == ((ccol // HEAD_DIM) % 2)
    ones_bf = jnp.ones((L, LANES), _BF16)
    rep = lane128 // N_HEADS

    def by_copy(pieces, first_copy, fill):
        out = jnp.full((L, LANES), fill, pieces[0].dtype)
        for k, piece in enumerate(pieces):
            out = jnp.where(rep == first_copy + k, piece, out)
        return out

    def sb_piece(i):
        c, half = 4 + i // 2, i % 2
        r = _dot(h, w_ref[:, c * D_GROUP + half * 2 * LANES:c * D_GROUP + (half + 1) * 2 * LANES])
        if c == 6:
            vs_ref[:, half * 2 * LANES:(half + 1) * 2 * LANES] = r.astype(_BF16)
            return
        gain_ref, out_ref = (qg_ref, qs_ref) if c == 4 else (kg_ref, ks_ref)
        for p in range(2):
            sl = slice((2 * half + p) * LANES, (2 * half + p + 1) * LANES)
            out_ref[:, sl] = _pair_rmsnorm(r[:, p * LANES:(p + 1) * LANES], gain_ref[...]).astype(_BF16)

    pieces_done = 0
    for c in range(ts // L):
        rows = slice(c * L, (c + 1) * L)
        gates = gate_ref[rows, :] + gb_ref[...]
        ic = gates[:, :LANES]
        fp = gates[:, LANES:]
        lf = jnp.minimum(fp, 0.0) - jnp.log(1.0 + jnp.exp(-jnp.abs(fp)))
        b = _dot(tril3, jnp.concatenate(_split_bf16(lf, 3), axis=0))
        r = ic - b
        run = r
        k = 1
        while k < L:
            run = jnp.maximum(run, jnp.where(row128 >= k, pltpu.roll(run, k, axis=0), _NEG))
            k *= 2
        m_prev = m_ref[...]
        big_m = jnp.maximum(run, m_prev)
        m = b + big_m
        s_inter = jnp.exp(m_prev - big_m)
        w_s = jnp.exp(r - big_m[L - 1:L, :])
        m_ref[...] = m[L - 1:L, :]

        lhs_e = by_copy(_split_bf16(-big_m, 3) + [ones_bf] * 3, 0, 0.0)
        r_pieces = by_copy([p.astype(_F32) for p in _split_bf16(r, 3)], 3, 0.0)
        r_t = jnp.concatenate([r_pieces.T.astype(_BF16)] * N_HEADS, axis=1)
        esel = esel_ref[...]
        rhs_e = jnp.where(esel == 1, jnp.ones_like(r_t), jnp.where(esel == 2, r_t, jnp.zeros_like(r_t)))
        e_all = _dot(lhs_e, rhs_e)
        lhs_b = by_copy(_split_bf16(s_inter, 3) + _split_bf16(w_s, 2) + _split_bf16(m, 3), 0, 0.0)
        bc = _dot(lhs_b, bsel_ref[...])

        for p in range(N_PAIRS):
            lanes = slice(p * LANES, (p + 1) * LANES)
            s2 = bc[:, 2 * p * LANES:2 * (p + 1) * LANES]
            w2 = bc[:, (2 * N_PAIRS + p) * LANES:(2 * N_PAIRS + p + 1) * LANES]
            m2 = bc[:, (3 * N_PAIRS + p) * LANES:(3 * N_PAIRS + p + 1) * LANES]
            q2 = qkb_ref[rows, p * LANES:(p + 1) * LANES]
            k2 = qkb_ref[rows, D_GROUP + p * LANES:D_GROUP + (p + 1) * LANES]
            vaug = jnp.concatenate([vm_ref[rows, lanes], ones_bf], axis=1)
            zk = jnp.zeros_like(k2)
            zv = jnp.zeros_like(vaug)
            k_cat = jnp.concatenate([jnp.where(first128, k2, zk), jnp.where(first128, zk, k2)], axis=0)
            v_cat = jnp.concatenate([jnp.where(first256, vaug, zv), jnp.where(first256, zv, vaug)], axis=0)
            e2 = e_all[:, 2 * p * L:2 * (p + 1) * L]
            w = (jnp.exp(jnp.where(tril2, e2, _NEG)) * _dot_nt(q2, k_cat)).astype(_BF16)
            c_old = c_ref[p]
            tot = s2 * _dot(q2, c_old.astype(_BF16)) + _dot(w, v_cat)
            den = jnp.maximum(jnp.abs(tot[:, LANES:]), jnp.exp(-m2))
            y = _pair_rmsnorm(tot[:, :LANES] / den, og_ref[:, lanes])
            o = om_ref[rows, lanes]
            y_ref[rows, lanes] = (y * (1.0 / (1.0 + jnp.exp(-o)))).astype(_BF16)
            kw = (k2.astype(_F32) * w2).astype(_BF16)
            c_ref[p] = s2[L - 1:L, :] * c_old + jnp.where(c_mask, _dot_tn(kw, vaug), 0.0)
            if p % 2 == 1 and pieces_done < 6:
                sb_piece(pieces_done)
                pieces_done += 1
    while pieces_done < 6:
        sb_piece(pieces_done)
        pieces_done += 1


def _mlstm_selectors():
    L = MLSTM_CHUNK
    row = jnp.arange(LANES)[:, None]
    head, rep = row % N_HEADS, row // N_HEADS
    col_head = jnp.arange(N_HEADS * L)[None, :] // L
    esel = jnp.where(head == col_head, jnp.where(rep < 3, 1, jnp.where(rep < 6, 2, 0)), 0).astype(_BF16)
    col = jnp.arange(4 * N_PAIRS * LANES)[None, :]
    grp = col // LANES
    pair = jnp.where(grp < 2 * N_PAIRS, grp // 2, (grp - 2 * N_PAIRS) % N_PAIRS)
    col_head = 2 * pair + (col % LANES) // HEAD_DIM
    src = jnp.where(grp < 2 * N_PAIRS, 0, jnp.where(grp < 3 * N_PAIRS, 1, 2))
    row_src = jnp.where(rep < 3, 0, jnp.where(rep < 5, 1, 2))
    bsel = ((head == col_head) & (row_src == src)).astype(_BF16)
    return esel, bsel


def _mix_in(layer, x, g, w, qg, kg, conv_w, conv_b, gate_b, out_g, esel, bsel):
    bsz, s, _ = x.shape
    ts = min(MLSTM_TILE, s)
    tok = lambda width: pl.BlockSpec((None, ts, width), lambda b, i: (b, i, 0))
    full = lambda shape: pl.BlockSpec(shape, lambda b, i: (0, 0))
    per_layer = lambda a: pl.BlockSpec((None,) + a.shape[1:], lambda b, i: (layer, 0, 0))
    act = jax.ShapeDtypeStruct((bsz, s, D_GROUP), _BF16)
    return pl.pallas_call(
        _mix_in_kernel,
        grid=(bsz, s // ts),
        in_specs=[tok(D_MODEL), per_layer(g), per_layer(w), per_layer(qg), per_layer(kg),
                  per_layer(conv_w), per_layer(conv_b), per_layer(gate_b), per_layer(out_g),
                  full(esel.shape), full(bsel.shape)],
        out_specs=[tok(D_GROUP)] * 4,
        out_shape=[act] * 4,
        scratch_shapes=[pltpu.VMEM((N_PAIRS, LANES, 2 * LANES), _F32),
                        pltpu.VMEM((1, LANES), _F32),
                        pltpu.VMEM((ts + 8, 2 * D_GROUP), _F32),
                        pltpu.VMEM((ts, 2 * D_GROUP), _BF16),
                        pltpu.VMEM((ts, D_GROUP), _BF16),
                        pltpu.VMEM((ts, D_GROUP), _F32),
                        pltpu.VMEM((ts, 2 * LANES), _F32)],
        compiler_params=pltpu.CompilerParams(dimension_semantics=("parallel", "arbitrary"),
                                             vmem_limit_bytes=VMEM_LIMIT),
        name="mix_in",
    )(x, g, w, qg, kg, conv_w, conv_b, gate_b, out_g, esel, bsel)


def _sb_kernel(q_ref, k_ref, v_ref, t_ref, og_ref, y_ref, acc_ref, tot_ref):
    tq = min(SB_TILE, q_ref.shape[0])
    group = q_ref.shape[0] // tq
    lane = lax.broadcasted_iota(jnp.int32, (tq, LANES), 1)
    first = lane < HEAD_DIM
    r_i = lax.broadcasted_iota(jnp.int32, (tq, tq), 0)
    c_i = lax.broadcasted_iota(jnp.int32, (tq, tq), 1)
    causal = c_i < r_i

    def scores(q_e, j):
        return _dot_nt(q_e, k_ref[pl.ds(pl.multiple_of(j * tq, tq), tq), :])

    def softplus_pieces(z, mask):
        sp = jnp.maximum(z, 0.0) + jnp.log(1.0 + jnp.exp2(-jnp.abs(z))) * LOG2E
        if mask is not None:
            sp = jnp.where(mask, sp, 0.0)
        return jnp.concatenate(_split_bf16(sp, SB_SPLIT), axis=1)

    def cumulative(pieces):
        return _dot(pieces, t_ref[...])

    def weights(z, cum, mask):
        d = z - cum
        if mask is not None:
            d = jnp.where(mask, d, _NEG)
        return jnp.exp2(d).astype(_BF16), jnp.broadcast_to(cum[:, 0:1], (tq, LANES))

    def values(w, j):
        return _dot(w, v_ref[pl.ds(pl.multiple_of(j * tq, tq), tq), :])

    def block_terms(q_e, j, mask):
        z = scores(q_e, j)
        w, rs = weights(z, cumulative(softplus_pieces(z, mask)), mask)
        return values(w, j), rs

    q_heads = []
    chains = []
    for g in range(group):
        qi = pl.program_id(2) * group + g
        q2 = q_ref[g * tq:(g + 1) * tq, :]
        zero = jnp.zeros_like(q2)
        q_heads.append((jnp.where(first, q2, zero), jnp.where(first, zero, q2)))
        for e in range(2):
            chains += [(g, e, qi, causal), (g, e, jnp.maximum(qi - 1, 0), None)]
    terms = {}
    for w0 in range(0, len(chains), SB_WAVE):
        wave = chains[w0:w0 + SB_WAVE]
        zs = [scores(q_heads[g][e], j) for g, e, j, _ in wave]
        pieces = [softplus_pieces(z, mask) for z, (_, _, _, mask) in zip(zs, wave)]
        cums = [cumulative(p) for p in pieces]
        ws = [weights(z, cum, mask) for z, cum, (_, _, _, mask) in zip(zs, cums, wave)]
        for (w, rs), (g, e, j, mask) in zip(ws, wave):
            terms[g, e, mask is None] = (values(w, j), rs)
    for g in range(group):
        has_prev = (pl.program_id(2) * group + g > 0).astype(_F32)
        for e in range(2):
            pv0, rs0 = terms[g, e, False]
            pv1, rs1 = terms[g, e, True]
            acc_ref[g, e] = pv0 + (has_prev * jnp.exp2(-rs0)) * pv1
            tot_ref[g, e] = rs0 + has_prev * rs1

    def more(c):
        return jnp.logical_and(c[0] >= 0, c[1] > 0)

    for g in range(group):
        qi = pl.program_id(2) * group + g

        def alive(g=g):
            return (jnp.min(jnp.minimum(tot_ref[g, 0], tot_ref[g, 1])) < SB_DEAD_LOG2).astype(jnp.int32)

        def sweep(c, g=g, alive=alive):
            j = c[0]
            for e in range(2):
                pv, rs = block_terms(q_heads[g][e], j, None)
                tot = tot_ref[g, e]
                acc_ref[g, e] = acc_ref[g, e] + jnp.exp2(-tot) * pv
                tot_ref[g, e] = tot + rs
            return j - 1, alive()

        lax.while_loop(more, sweep, (qi - 2, alive()))
        y = jnp.where(first, acc_ref[g, 0], acc_ref[g, 1])
        y_ref[g * tq:(g + 1) * tq, :] = _pair_rmsnorm(y, og_ref[...]).astype(_BF16)


def _sb_attn(layer, qs, ks, vs, out_g):
    bsz, s, _ = qs.shape
    tq = min(SB_TILE, s)
    group = SB_GROUP if s % (SB_GROUP * tq) == 0 else 1
    rows = group * tq
    tri = (jnp.arange(tq)[:, None] >= jnp.arange(tq)[None, :]).astype(_BF16)
    t_mat = jnp.concatenate([tri] * SB_SPLIT, axis=0)
    return pl.pallas_call(
        _sb_kernel,
        grid=(bsz, N_PAIRS, s // rows),
        in_specs=[pl.BlockSpec((None, rows, LANES), lambda b, p, i: (b, i, p)),
                  pl.BlockSpec((None, s, LANES), lambda b, p, i: (b, 0, p)),
                  pl.BlockSpec((None, s, LANES), lambda b, p, i: (b, 0, p)),
                  pl.BlockSpec((SB_SPLIT * tq, tq), lambda b, p, i: (0, 0)),
                  pl.BlockSpec((None, 1, LANES), lambda b, p, i: (layer, 0, N_PAIRS + p))],
        out_specs=pl.BlockSpec((None, rows, LANES), lambda b, p, i: (b, i, p)),
        out_shape=jax.ShapeDtypeStruct((bsz, s, D_GROUP), _BF16),
        scratch_shapes=[pltpu.VMEM((group, 2, tq, LANES), _F32),
                        pltpu.VMEM((group, 2, tq, LANES), _F32)],
        compiler_params=pltpu.CompilerParams(dimension_semantics=("parallel", "parallel", "arbitrary"),
                                             vmem_limit_bytes=VMEM_LIMIT),
        name="sb_attn",
    )(qs, ks, vs, t_mat, out_g)


def _out_mlp_kernel(x_ref, ym_ref, ys_ref, wo_ref, g_ref, wu_ref, wd_ref, o_ref, a_ref):
    x = x_ref[...] + _dot(ym_ref[...], wo_ref[0:D_GROUP, :]) + _dot(ys_ref[...], wo_ref[D_GROUP:, :])
    ms = jnp.mean(x * x, axis=-1, keepdims=True)
    h = (x * lax.rsqrt(ms + NORM_EPS) * g_ref[...]).astype(_BF16)
    for c in range(D_FF // FF_TILE):
        cols = slice(c * FF_TILE, (c + 1) * FF_TILE)
        up = jnp.maximum(_dot(h, wu_ref[:, cols]), 0.0)
        a_ref[:, cols] = (up * up).astype(_BF16)
    o_ref[...] = x + _dot(a_ref[...], wd_ref[...])


def _out_mlp(layer, x2d, ym, ys, wo, g, wu, wd):
    n = x2d.shape[0]
    tn = TOK_TILE
    row = lambda width: pl.BlockSpec((tn, width), lambda i: (i, 0))
    per_layer = lambda a: pl.BlockSpec((None,) + a.shape[1:], lambda i: (layer, 0, 0))
    return pl.pallas_call(
        _out_mlp_kernel,
        grid=(n // tn,),
        in_specs=[row(D_MODEL), row(D_GROUP), row(D_GROUP), per_layer(wo), per_layer(g), per_layer(wu), per_layer(wd)],
        out_specs=row(D_MODEL),
        out_shape=jax.ShapeDtypeStruct((n, D_MODEL), _F32),
        scratch_shapes=[pltpu.VMEM((tn, D_FF), _BF16)],
        compiler_params=pltpu.CompilerParams(dimension_semantics=("parallel",),
                                             vmem_limit_bytes=VMEM_LIMIT),
        name="out_mlp",
    )(x2d, ym, ys, wo, g, wu, wd)


def _pad_last(a, width):
    return jnp.pad(a, [(0, 0)] * (a.ndim - 1) + [(0, width - a.shape[-1])])


def kernel(x, attn_norm_g, w_in, conv_w, conv_b, b_igate, b_fgate, q_norm_g, k_norm_g, out_norm_g, w_out,
           mlp_norm_g, w_up, w_down):
    bsz, s, _ = x.shape
    n = bsz * s
    depth = w_in.shape[0]
    o_gate = 4 * D_GROUP
    o_sb = o_gate + 2 * N_HEADS
    gate_copies = lambda a: _pad_last(jnp.tile(a, (1,) * (a.ndim - 1) + (GATE_REP,)), LANES)
    w = jnp.concatenate([w_in[:, :, :o_gate], w_in[:, :, o_sb:], gate_copies(w_in[:, :, o_gate:o_gate + N_HEADS]),
                         gate_copies(w_in[:, :, o_gate + N_HEADS:o_sb])], axis=2).astype(_BF16)
    vec = lambda a: a[:, None, :]
    qg = vec(jnp.tile(q_norm_g * (HEAD_DIM ** -0.5 * LOG2E), (1, 2)))
    kg = vec(jnp.tile(k_norm_g, (1, 2)))
    gate_b = vec(jnp.concatenate([gate_copies(b_igate), gate_copies(b_fgate)], axis=1))
    wo, wu, wd = w_out.astype(_BF16), w_up.astype(_BF16), w_down.astype(_BF16)
    esel, bsel = _mlstm_selectors()
    x2d = x.reshape(n, D_MODEL)
    for l in range(depth):
        ym, qs, ks, vs = _mix_in(l, x2d.reshape(bsz, s, D_MODEL), vec(attn_norm_g), w, qg, kg, conv_w, vec(conv_b),
                                 gate_b, vec(out_norm_g), esel, bsel)
        ys = _sb_attn(l, qs, ks, vs, vec(out_norm_g))
        x2d = _out_mlp(l, x2d, ym.reshape(n, D_GROUP), ys.reshape(n, D_GROUP), wo, vec(mlp_norm_g), wu, wd)
    return x2d.reshape(bsz, s, D_MODEL)
```

```python
import jax
import jax.numpy as jnp
from jax import lax
from jax.experimental import pallas as pl
from jax.experimental.pallas import tpu as pltpu

D_MODEL = 1024
HEAD_DIM = 64
N_HEADS = 8
D_GROUP = N_HEADS * HEAD_DIM
N_PAIRS = N_HEADS // 2
D_FF = 4 * D_MODEL
CONV_WIDTH = 4
NORM_EPS = 1e-6
LANES = 128
SUBLANES = 8
HALO = SUBLANES
D_PROJ = 7 * D_GROUP + 2 * LANES

TOK_TILE = 512
FF_TILE = 512
MLSTM_TILE = 512
MLSTM_CHUNK = 128
GATE_REP = 8
SB_TILE = 256
SB_GROUP = 8
SB_WAVE = 8
SB_SPLIT = 1
SB_DEAD_LOG2 = 160.0
LOG2E = 1.4426950408889634
VMEM_LIMIT = 56 * 1024 * 1024

_F32 = jnp.float32
_BF16 = jnp.bfloat16
_NEG = -1e30


def _dot(a, b):
    return jnp.dot(a, b, preferred_element_type=_F32)


def _dot_nt(a, b):
    return lax.dot_general(a, b, (((1,), (1,)), ((), ())), preferred_element_type=_F32)


def _dot_tn(a, b):
    return lax.dot_general(a, b, (((0,), (0,)), ((), ())), preferred_element_type=_F32)


def _split_bf16(a, pieces):
    parts = []
    r = a
    for i in range(pieces):
        p = r.astype(_BF16)
        parts.append(p)
        if i + 1 < pieces:
            r = r - p.astype(_F32)
    return parts


def _pair_rmsnorm(x, gain):
    lane = lax.broadcasted_iota(jnp.int32, x.shape, 1)
    first = lane < HEAD_DIM
    sq = x * x
    s_a = jnp.sum(jnp.where(first, sq, 0.0), axis=-1, keepdims=True)
    s_b = jnp.sum(jnp.where(first, 0.0, sq), axis=-1, keepdims=True)
    ms = jnp.where(first, s_a, s_b) * (1.0 / HEAD_DIM)
    return x * lax.rsqrt(ms + NORM_EPS) * gain


def _mix_in_kernel(x_ref, g_ref, w_ref, qg_ref, kg_ref, cw_ref, cb_ref, gb_ref, og_ref, esel_ref, bsel_ref,
                   y_ref, qs_ref, ks_ref, vs_ref,
                   c_ref, m_ref, ubuf_ref, qkb_ref, vm_ref, om_ref, gate_ref):
    ts = x_ref.shape[0]
    L = MLSTM_CHUNK

    @pl.when(pl.program_id(1) == 0)
    def _():
        c_ref[...] = jnp.zeros_like(c_ref)
        m_ref[...] = jnp.zeros_like(m_ref)
        ubuf_ref[0:HALO, :] = jnp.zeros((HALO, 2 * D_GROUP), _F32)

    x = x_ref[...]
    ms = jnp.mean(x * x, axis=-1, keepdims=True)
    h = (x * lax.rsqrt(ms + NORM_EPS) * g_ref[...]).astype(_BF16)

    def proj(c):
        return _dot(h, w_ref[:, c * D_GROUP:(c + 1) * D_GROUP])

    ubuf_ref[HALO:, 0:D_GROUP] = proj(0)
    ubuf_ref[HALO:, D_GROUP:2 * D_GROUP] = proj(1)
    gate_ref[...] = _dot(h, w_ref[:, 7 * D_GROUP:D_PROJ])

    acc = cb_ref[...] + ubuf_ref[HALO:, :] * cw_ref[CONV_WIDTH - 1:CONV_WIDTH, :]
    for j in range(1, CONV_WIDTH):
        acc = acc + ubuf_ref[HALO - j:HALO - j + ts, :] * cw_ref[CONV_WIDTH - 1 - j:CONV_WIDTH - j, :]
    ubuf_ref[0:HALO, :] = ubuf_ref[ts:ts + HALO, :]
    act = acc * (1.0 / (1.0 + jnp.exp(-acc)))
    qkb_ref[:, :D_GROUP] = (act[:, :D_GROUP] * (HEAD_DIM ** -0.5)).astype(_BF16)
    qkb_ref[:, D_GROUP:] = act[:, D_GROUP:].astype(_BF16)
    vm_ref[...] = proj(2).astype(_BF16)
    om_ref[...] = proj(3)

    r_i = lax.broadcasted_iota(jnp.int32, (L, L), 0)
    c_i = lax.broadcasted_iota(jnp.int32, (L, L), 1)
    tril_bf = jnp.where(c_i <= r_i, 1.0, 0.0).astype(_BF16)
    tril3 = jnp.concatenate([tril_bf] * 3, axis=1)
    r2 = lax.broadcasted_iota(jnp.int32, (L, 2 * L), 0)
    c2 = lax.broadcasted_iota(jnp.int32, (L, 2 * L), 1)
    tril2 = (c2 % L) <= r2
    row128 = lax.broadcasted_iota(jnp.int32, (L, LANES), 0)
    lane128 = lax.broadcasted_iota(jnp.int32, (L, LANES), 1)
    lane256 = lax.broadcasted_iota(jnp.int32, (L, 2 * LANES), 1)
    first128 = lane128 < HEAD_DIM
    first256 = (lane256 // HEAD_DIM) % 2 == 0
    crow = lax.broadcasted_iota(jnp.int32, (LANES, 2 * LANES), 0)
    ccol = lax.broadcasted_iota(jnp.int32, (LANES, 2 * LANES), 1)
    c_mask = (crow // HEAD_DIM) == ((ccol // HEAD_DIM) % 2)
    ones_bf = jnp.ones((L, LANES), _BF16)
    rep = lane128 // N_HEADS

    def by_copy(pieces, first_copy, fill):
        out = jnp.full((L, LANES), fill, pieces[0].dtype)
        for k, piece in enumerate(pieces):
            out = jnp.where(rep == first_copy + k, piece, out)
        return out

    def sb_piece(i):
        c, half = 4 + i // 2, i % 2
        r = _dot(h, w_ref[:, c * D_GROUP + half * 2 * LANES:c * D_GROUP + (half + 1) * 2 * LANES])
        if c == 6:
            vs_ref[:, half * 2 * LANES:(half + 1) * 2 * LANES] = r.astype(_BF16)
            return
        gain_ref, out_ref = (qg_ref, qs_ref) if c == 4 else (kg_ref, ks_ref)
        for p in range(2):
            sl = slice((2 * half + p) * LANES, (2 * half + p + 1) * LANES)
            out_ref[:, sl] = _pair_rmsnorm(r[:, p * LANES:(p + 1) * LANES], gain_ref[...]).astype(_BF16)

    pieces_done = 0
    gate_stage = []
    for c in range(ts // L):
        rows = slice(c * L, (c + 1) * L)
        gates = gate_ref[rows, :] + gb_ref[...]
        ic = gates[:, :LANES]
        fp = gates[:, LANES:]
        lf = jnp.minimum(fp, 0.0) - jnp.log(1.0 + jnp.exp(-jnp.abs(fp)))
        b = _dot(tril3, jnp.concatenate(_split_bf16(lf, 3), axis=0))
        r = ic - b
        run = r
        k = 1
        while k < L:
            run = jnp.maximum(run, jnp.where(row128 >= k, pltpu.roll(run, k, axis=0), _NEG))
            k *= 2
        m_prev = m_ref[...]
        big_m = jnp.maximum(run, m_prev)
        m = b + big_m
        s_inter = jnp.exp(m_prev - big_m)
        w_s = jnp.exp(r - big_m[L - 1:L, :])
        m_ref[...] = m[L - 1:L, :]

        lhs_e = by_copy(_split_bf16(-big_m, 3) + [ones_bf] * 3, 0, 0.0)
        r_pieces = by_copy([p.astype(_F32) for p in _split_bf16(r, 3)], 3, 0.0)
        r_t = jnp.concatenate([r_pieces.T.astype(_BF16)] * N_HEADS, axis=1)
        esel = esel_ref[...]
        rhs_e = jnp.where(esel == 1, jnp.ones_like(r_t), jnp.where(esel == 2, r_t, jnp.zeros_like(r_t)))
        e_all = _dot(lhs_e, rhs_e)
        lhs_b = by_copy(_split_bf16(s_inter, 3) + _split_bf16(w_s, 2) + _split_bf16(m, 3), 0, 0.0)
        bc = _dot(lhs_b, bsel_ref[...])

        gate_stage.append((rows, bc, e_all))

    prep_all = []
    for rows, _, _ in gate_stage:
        prep = []
        for p in range(N_PAIRS):
            lanes = slice(p * LANES, (p + 1) * LANES)
            q2 = qkb_ref[rows, p * LANES:(p + 1) * LANES]
            k2 = qkb_ref[rows, D_GROUP + p * LANES:D_GROUP + (p + 1) * LANES]
            vaug = jnp.concatenate([vm_ref[rows, lanes], ones_bf], axis=1)
            zk = jnp.zeros_like(k2)
            zv = jnp.zeros_like(vaug)
            k_cat = jnp.concatenate([jnp.where(first128, k2, zk), jnp.where(first128, zk, k2)], axis=0)
            v_cat = jnp.concatenate([jnp.where(first256, vaug, zv), jnp.where(first256, zv, vaug)], axis=0)
            prep.append((q2, k2, vaug, v_cat, _dot_nt(q2, k_cat)))
        prep_all.append(prep)
    ws_all = []
    for (_, _, e_all), prep in zip(gate_stage, prep_all):
        ws_all.append([(jnp.exp(jnp.where(tril2, e_all[:, 2 * p * L:2 * (p + 1) * L], _NEG)) * prep[p][4]).astype(_BF16)
                       for p in range(N_PAIRS)])
    staged = []
    for (rows, bc, _), prep, ws in zip(gate_stage, prep_all, ws_all):
        staged.append((rows, bc, prep, [_dot(ws[p], prep[p][3]) for p in range(N_PAIRS)]))

    for rows, bc, prep, intra in staged:
        inter = []
        for p in range(N_PAIRS):
            c_old = c_ref[p]
            inter.append((c_old, _dot(prep[p][0], c_old.astype(_BF16))))
        for p in range(N_PAIRS):
            lanes = slice(p * LANES, (p + 1) * LANES)
            s2 = bc[:, 2 * p * LANES:2 * (p + 1) * LANES]
            m2 = bc[:, (3 * N_PAIRS + p) * LANES:(3 * N_PAIRS + p + 1) * LANES]
            tot = s2 * inter[p][1] + intra[p]
            den = jnp.maximum(jnp.abs(tot[:, LANES:]), jnp.exp(-m2))
            y = _pair_rmsnorm(tot[:, :LANES] / den, og_ref[:, lanes])
            o = om_ref[rows, lanes]
            y_ref[rows, lanes] = (y * (1.0 / (1.0 + jnp.exp(-o)))).astype(_BF16)
        for p in range(N_PAIRS):
            s2 = bc[:, 2 * p * LANES:2 * (p + 1) * LANES]
            w2 = bc[:, (2 * N_PAIRS + p) * LANES:(2 * N_PAIRS + p + 1) * LANES]
            kw = (prep[p][1].astype(_F32) * w2).astype(_BF16)
            c_ref[p] = s2[L - 1:L, :] * inter[p][0] + jnp.where(c_mask, _dot_tn(kw, prep[p][2]), 0.0)
            if p % 2 == 1 and pieces_done < 6:
                sb_piece(pieces_done)
                pieces_done += 1
    while pieces_done < 6:
        sb_piece(pieces_done)
        pieces_done += 1


def _mlstm_selectors():
    L = MLSTM_CHUNK
    row = jnp.arange(LANES)[:, None]
    head, rep = row % N_HEADS, row // N_HEADS
    col_head = jnp.arange(N_HEADS * L)[None, :] // L
    esel = jnp.where(head == col_head, jnp.where(rep < 3, 1, jnp.where(rep < 6, 2, 0)), 0).astype(_BF16)
    col = jnp.arange(4 * N_PAIRS * LANES)[None, :]
    grp = col // LANES
    pair = jnp.where(grp < 2 * N_PAIRS, grp // 2, (grp - 2 * N_PAIRS) % N_PAIRS)
    col_head = 2 * pair + (col % LANES) // HEAD_DIM
    src = jnp.where(grp < 2 * N_PAIRS, 0, jnp.where(grp < 3 * N_PAIRS, 1, 2))
    row_src = jnp.where(rep < 3, 0, jnp.where(rep < 5, 1, 2))
    bsel = ((head == col_head) & (row_src == src)).astype(_BF16)
    return esel, bsel


def _mix_in(layer, x, g, w, qg, kg, conv_w, conv_b, gate_b, out_g, esel, bsel):
    bsz, s, _ = x.shape
    ts = min(MLSTM_TILE, s)
    tok = lambda width: pl.BlockSpec((None, ts, width), lambda b, i: (b, i, 0))
    full = lambda shape: pl.BlockSpec(shape, lambda b, i: (0, 0))
    per_layer = lambda a: pl.BlockSpec((None,) + a.shape[1:], lambda b, i: (layer, 0, 0))
    act = jax.ShapeDtypeStruct((bsz, s, D_GROUP), _BF16)
    return pl.pallas_call(
        _mix_in_kernel,
        grid=(bsz, s // ts),
        in_specs=[tok(D_MODEL), per_layer(g), per_layer(w), per_layer(qg), per_layer(kg),
                  per_layer(conv_w), per_layer(conv_b), per_layer(gate_b), per_layer(out_g),
                  full(esel.shape), full(bsel.shape)],
        out_specs=[tok(D_GROUP)] * 4,
        out_shape=[act] * 4,
        scratch_shapes=[pltpu.VMEM((N_PAIRS, LANES, 2 * LANES), _F32),
                        pltpu.VMEM((1, LANES), _F32),
                        pltpu.VMEM((ts + HALO, 2 * D_GROUP), _F32),
                        pltpu.VMEM((ts, 2 * D_GROUP), _BF16),
                        pltpu.VMEM((ts, D_GROUP), _BF16),
                        pltpu.VMEM((ts, D_GROUP), _F32),
                        pltpu.VMEM((ts, 2 * LANES), _F32)],
        compiler_params=pltpu.CompilerParams(dimension_semantics=("parallel", "arbitrary"),
                                             vmem_limit_bytes=VMEM_LIMIT),
        name="mix_in",
    )(x, g, w, qg, kg, conv_w, conv_b, gate_b, out_g, esel, bsel)


def _sb_kernel(q_ref, k_ref, v_ref, t_ref, og_ref, y_ref, acc_ref, tot_ref):
    tq = min(SB_TILE, q_ref.shape[0])
    group = q_ref.shape[0] // tq
    lane = lax.broadcasted_iota(jnp.int32, (tq, LANES), 1)
    first = lane < HEAD_DIM
    r_i = lax.broadcasted_iota(jnp.int32, (tq, tq), 0)
    c_i = lax.broadcasted_iota(jnp.int32, (tq, tq), 1)
    causal = c_i < r_i

    def scores(q_e, j):
        return _dot_nt(q_e, k_ref[pl.ds(pl.multiple_of(j * tq, tq), tq), :])

    def softplus_pieces(z, mask):
        sp = jnp.maximum(z, 0.0) + jnp.log(1.0 + jnp.exp2(-jnp.abs(z))) * LOG2E
        if mask is not None:
            sp = jnp.where(mask, sp, 0.0)
        return jnp.concatenate(_split_bf16(sp, SB_SPLIT), axis=1)

    def cumulative(pieces):
        return _dot(pieces, t_ref[...])

    def weights(z, cum, mask):
        d = z - cum
        if mask is not None:
            d = jnp.where(mask, d, _NEG)
        return jnp.exp2(d).astype(_BF16), jnp.broadcast_to(cum[:, 0:1], (tq, LANES))

    def values(w, j):
        return _dot(w, v_ref[pl.ds(pl.multiple_of(j * tq, tq), tq), :])

    def block_terms(q_e, j, mask):
        z = scores(q_e, j)
        w, rs = weights(z, cumulative(softplus_pieces(z, mask)), mask)
        return values(w, j), rs

    q_heads = []
    chains = []
    for g in range(group):
        qi = pl.program_id(2) * group + g
        q2 = q_ref[g * tq:(g + 1) * tq, :]
        zero = jnp.zeros_like(q2)
        q_heads.append((jnp.where(first, q2, zero), jnp.where(first, zero, q2)))
        for e in range(2):
            chains += [(g, e, qi, causal), (g, e, jnp.maximum(qi - 1, 0), None)]
    terms = {}
    for w0 in range(0, len(chains), SB_WAVE):
        wave = chains[w0:w0 + SB_WAVE]
        zs = [scores(q_heads[g][e], j) for g, e, j, _ in wave]
        pieces = [softplus_pieces(z, mask) for z, (_, _, _, mask) in zip(zs, wave)]
        cums = [cumulative(p) for p in pieces]
        ws = [weights(z, cum, mask) for z, cum, (_, _, _, mask) in zip(zs, cums, wave)]
        for (w, rs), (g, e, j, mask) in zip(ws, wave):
            terms[g, e, mask is None] = (values(w, j), rs)
    for g in range(group):
        has_prev = (pl.program_id(2) * group + g > 0).astype(_F32)
        for e in range(2):
            pv0, rs0 = terms[g, e, False]
            pv1, rs1 = terms[g, e, True]
            acc_ref[g, e] = pv0 + (has_prev * jnp.exp2(-rs0)) * pv1
            tot_ref[g, e] = rs0 + has_prev * rs1

    def more(c):
        return jnp.logical_and(c[0] >= 0, c[1] > 0)

    for g in range(group):
        qi = pl.program_id(2) * group + g

        def alive(g=g):
            return (jnp.min(jnp.minimum(tot_ref[g, 0], tot_ref[g, 1])) < SB_DEAD_LOG2).astype(jnp.int32)

        def sweep(c, g=g, alive=alive):
            j = c[0]
            for e in range(2):
                pv, rs = block_terms(q_heads[g][e], j, None)
                tot = tot_ref[g, e]
                acc_ref[g, e] = acc_ref[g, e] + jnp.exp2(-tot) * pv
                tot_ref[g, e] = tot + rs
            return j - 1, alive()

        lax.while_loop(more, sweep, (qi - 2, alive()))
        y = jnp.where(first, acc_ref[g, 0], acc_ref[g, 1])
        y_ref[g * tq:(g + 1) * tq, :] = _pair_rmsnorm(y, og_ref[...]).astype(_BF16)


def _sb_attn(layer, qs, ks, vs, out_g):
    bsz, s, _ = qs.shape
    tq = min(SB_TILE, s)
    group = SB_GROUP if s % (SB_GROUP * tq) == 0 else 1
    rows = group * tq
    tri = (jnp.arange(tq)[:, None] >= jnp.arange(tq)[None, :]).astype(_BF16)
    t_mat = jnp.concatenate([tri] * SB_SPLIT, axis=0)
    return pl.pallas_call(
        _sb_kernel,
        grid=(bsz, N_PAIRS, s // rows),
        in_specs=[pl.BlockSpec((None, rows, LANES), lambda b, p, i: (b, i, p)),
                  pl.BlockSpec((None, s, LANES), lambda b, p, i: (b, 0, p)),
                  pl.BlockSpec((None, s, LANES), lambda b, p, i: (b, 0, p)),
                  pl.BlockSpec((SB_SPLIT * tq, tq), lambda b, p, i: (0, 0)),
                  pl.BlockSpec((None, 1, LANES), lambda b, p, i: (layer, 0, N_PAIRS + p))],
        out_specs=pl.BlockSpec((None, rows, LANES), lambda b, p, i: (b, i, p)),
        out_shape=jax.ShapeDtypeStruct((bsz, s, D_GROUP), _BF16),
        scratch_shapes=[pltpu.VMEM((group, 2, tq, LANES), _F32),
                        pltpu.VMEM((group, 2, tq, LANES), _F32)],
        compiler_params=pltpu.CompilerParams(dimension_semantics=("parallel", "parallel", "arbitrary"),
                                             vmem_limit_bytes=VMEM_LIMIT),
        name="sb_attn",
    )(qs, ks, vs, t_mat, out_g)


def _out_mlp_kernel(x_ref, ym_ref, ys_ref, wo_ref, g_ref, wu_ref, wd_ref, o_ref, a_ref):
    x = x_ref[...] + _dot(ym_ref[...], wo_ref[0:D_GROUP, :]) + _dot(ys_ref[...], wo_ref[D_GROUP:, :])
    ms = jnp.mean(x * x, axis=-1, keepdims=True)
    h = (x * lax.rsqrt(ms + NORM_EPS) * g_ref[...]).astype(_BF16)
    for c in range(D_FF // FF_TILE):
        cols = slice(c * FF_TILE, (c + 1) * FF_TILE)
        up = jnp.maximum(_dot(h, wu_ref[:, cols]), 0.0)
        a_ref[:, cols] = (up * up).astype(_BF16)
    o_ref[...] = x + _dot(a_ref[...], wd_ref[...])


def _out_mlp(layer, x2d, ym, ys, wo, g, wu, wd):
    n = x2d.shape[0]
    tn = TOK_TILE
    row = lambda width: pl.BlockSpec((tn, width), lambda i: (i, 0))
    per_layer = lambda a: pl.BlockSpec((None,) + a.shape[1:], lambda i: (layer, 0, 0))
    return pl.pallas_call(
        _out_mlp_kernel,
        grid=(n // tn,),
        in_specs=[row(D_MODEL), row(D_GROUP), row(D_GROUP), per_layer(wo), per_layer(g), per_layer(wu), per_layer(wd)],
        out_specs=row(D_MODEL),
        out_shape=jax.ShapeDtypeStruct((n, D_MODEL), _F32),
        scratch_shapes=[pltpu.VMEM((tn, D_FF), _BF16)],
        compiler_params=pltpu.CompilerParams(dimension_semantics=("parallel",),
                                             vmem_limit_bytes=VMEM_LIMIT),
        name="out_mlp",
    )(x2d, ym, ys, wo, g, wu, wd)


def _pad_last(a, width):
    return jnp.pad(a, [(0, 0)] * (a.ndim - 1) + [(0, width - a.shape[-1])])


def kernel(x, attn_norm_g, w_in, conv_w, conv_b, b_igate, b_fgate, q_norm_g, k_norm_g, out_norm_g, w_out,
           mlp_norm_g, w_up, w_down):
    bsz, s, _ = x.shape
    n = bsz * s
    depth = w_in.shape[0]
    o_gate = 4 * D_GROUP
    o_sb = o_gate + 2 * N_HEADS
    gate_copies = lambda a: _pad_last(jnp.tile(a, (1,) * (a.ndim - 1) + (GATE_REP,)), LANES)
    w = jnp.concatenate([w_in[:, :, :o_gate], w_in[:, :, o_sb:], gate_copies(w_in[:, :, o_gate:o_gate + N_HEADS]),
                         gate_copies(w_in[:, :, o_gate + N_HEADS:o_sb])], axis=2).astype(_BF16)
    vec = lambda a: a[:, None, :]
    qg = vec(jnp.tile(q_norm_g * (HEAD_DIM ** -0.5 * LOG2E), (1, 2)))
    kg = vec(jnp.tile(k_norm_g, (1, 2)))
    gate_b = vec(jnp.concatenate([gate_copies(b_igate), gate_copies(b_fgate)], axis=1))
    wo, wu, wd = w_out.astype(_BF16), w_up.astype(_BF16), w_down.astype(_BF16)
    esel, bsel = _mlstm_selectors()
    x2d = x.reshape(n, D_MODEL)
    for l in range(depth):
        ym, qs, ks, vs = _mix_in(l, x2d.reshape(bsz, s, D_MODEL), vec(attn_norm_g), w, qg, kg, conv_w, vec(conv_b),
                                 gate_b, vec(out_norm_g), esel, bsel)
        ys = _sb_attn(l, qs, ks, vs, vec(out_norm_g))
        x2d = _out_mlp(l, x2d, ym.reshape(n, D_GROUP), ys.reshape(n, D_GROUP), wo, vec(mlp_norm_g), wu, wd)
    return x2d.reshape(bsz, s, D_MODEL)
```
